```python
import math
import jax
import jax.numpy as jnp
from jax import lax
import numpy as np

D_MODEL = 4096
BATCH = 16
SEQ = 256
DEPTH = 2
DEC_BATCH = 4
DEC_SEQ = 1024
PAST_LEN = 512

GRID_W = 64
H_A = 12
DK_A = 128
DV_A = 128
W_A = H_A * DV_A
CHUNK = 64
SHORT_CONV = 3
H_B = 6
DK_B = 128
DV_B = 2 * DK_B
W_B = H_B * DV_B
Q_BLOCK = 128
ROPE_BASE = 10000.0
C_CH = 1024
HYENA_ORDER = 2
FILT_BANDS = 16
FILT_EMB = 1 + 2 * FILT_BANDS
FILT_HIDDEN = 64
FILT_OUT_STD = 0.02
HYENA_FAST_DECAY = 0.3
HYENA_SLOW_DECAY = 1.5
HYENA_TARGET = 1e-2
N_EXPERTS = 32
TOP_K = 4
D_FF = 1024
SWIGLU_LIMIT = 7.0
SWIGLU_ALPHA = 1.702
N_MOD = 6
EPS = 1e-6
IN_SIZES = (H_A * DK_A, H_A * DK_A, W_A, W_A, 2 * H_A, 2 * H_A,
            2 * H_B * DK_B, 2 * H_B * DK_B, W_B, (HYENA_ORDER + 1) * C_CH, 3 * D_MODEL)
N_IN = sum(IN_SIZES)

kernel_name = 'hybrid_diffusion_gdn_diffattn_hyena_moe_step'

F32 = jnp.float32


def rmsnorm(x, g, eps=EPS):
    x32 = x.astype(F32)
    y = x32 * lax.rsqrt(jnp.mean(x32 * x32, axis=-1, keepdims=True) + eps)
    return (y * g.astype(F32)).astype(x.dtype)


def l2norm(x):
    x32 = x.astype(F32)
    return x32 * lax.rsqrt(jnp.sum(x32 * x32, axis=-1, keepdims=True) + 1e-6)


def split_columns(proj):
    out, start = [], 0
    for size in IN_SIZES:
        out.append(proj[..., start:start + size])
        start += size
    return out


def depthwise_conv_centred(x, w):
    k = w.shape[0]
    return lax.conv_general_dilated(
        x, w[:, None, :].astype(x.dtype), window_strides=(1,),
        padding=[(k // 2, k // 2)], dimension_numbers=('NWC', 'WIO', 'NWC'),
        feature_group_count=x.shape[-1])


def gated_delta_chunked(q, k, v, g, beta, state0):
    bsz, n_tok, nh, _ = k.shape
    dv = v.shape[-1]
    n_chunks = n_tok // CHUNK

    def chunks(t):
        return t.reshape(bsz, n_chunks, CHUNK, nh, -1).transpose(1, 0, 3, 2, 4)

    qc, kc, vc = chunks(q), chunks(k), chunks(v)
    gc = chunks(g[..., None])[..., 0]
    bc = chunks(beta[..., None])[..., 0]
    gcum = jnp.cumsum(gc, axis=-1)
    incl = jnp.tril(jnp.ones((CHUNK, CHUNK), bool))
    strict = jnp.tril(jnp.ones((CHUNK, CHUNK), bool), -1)
    decay = jnp.where(incl, jnp.exp(jnp.where(incl, gcum[..., :, None] - gcum[..., None, :], 0.0)), 0.0)
    kbeta = kc * bc[..., None]
    a_mat = jnp.where(strict, jnp.einsum('nbhid,nbhjd->nbhij', kbeta, kc) * decay, 0.0)
    eye = jnp.eye(CHUNK, dtype=F32)
    t_mat = lax.linalg.triangular_solve(eye + a_mat, jnp.broadcast_to(eye, a_mat.shape),
                                        left_side=True, lower=True, unit_diagonal=True)
    u = t_mat @ (vc * bc[..., None])
    w = t_mat @ (kbeta * jnp.exp(gcum)[..., None])
    qk = jnp.where(incl, jnp.einsum('nbhid,nbhjd->nbhij', qc, kc) * decay, 0.0)

    def step(s, xs):
        q_i, k_i, u_i, w_i, qk_i, g_i = xs
        v_new = u_i - w_i @ s
        o_i = (q_i * jnp.exp(g_i)[..., None]) @ s + qk_i @ v_new
        g_last = g_i[..., -1:]
        s = s * jnp.exp(g_last)[..., None] + jnp.einsum(
            'bhcd,bhce->bhde', k_i * jnp.exp(g_last - g_i)[..., None], v_new)
        return s, o_i

    s_fin, o = lax.scan(step, state0.astype(F32), (qc, kc, u, w, qk, gcum))
    o = o.transpose(1, 0, 3, 2, 4).reshape(bsz, n_tok, nh, dv)
    return o, s_fin


def mixer_gdn(qa, ka, va, za, aa, ba, conv_w, a_log, dt_bias, onorm_g, s0_f, s0_b):
    bsz, n_tok, _ = qa.shape
    qkv = jax.nn.silu(depthwise_conv_centred(jnp.concatenate([qa, ka, va], axis=-1), conv_w))
    q, k, v = jnp.split(qkv, 3, axis=-1)
    q = l2norm(q.reshape(bsz, n_tok, H_A, DK_A)) * (DK_A ** -0.5)
    k = l2norm(k.reshape(bsz, n_tok, H_A, DK_A))
    v = v.reshape(bsz, n_tok, H_A, DV_A).astype(F32)
    a = aa.astype(F32).reshape(bsz, n_tok, 2, H_A)
    g = -jnp.exp(a_log.astype(F32)) * jax.nn.softplus(a + dt_bias.astype(F32))
    beta = jax.nn.sigmoid(ba.astype(F32).reshape(bsz, n_tok, 2, H_A))
    if s0_f is None:
        s0_f = jnp.zeros((bsz, H_A, DK_A, DV_A), F32)
        s0_b = jnp.zeros((bsz, H_A, DK_A, DV_A), F32)
    o_f, s_f = gated_delta_chunked(q, k, v, g[:, :, 0], beta[:, :, 0], s0_f)
    rev = lambda t: jnp.flip(t, axis=1)
    o_b, s_b = gated_delta_chunked(rev(q), rev(k), rev(v), rev(g[:, :, 1]), rev(beta[:, :, 1]), s0_b)
    o = rmsnorm(o_f + rev(o_b), onorm_g) * jax.nn.silu(za.astype(F32).reshape(bsz, n_tok, H_A, DV_A))
    return o.reshape(bsz, n_tok, W_A).astype(qa.dtype), s_f, s_b


def axial_rope_tables(n_tok):
    rows = n_tok // GRID_W
    row = jnp.repeat(jnp.arange(rows), GRID_W)
    col = jnp.tile(jnp.arange(GRID_W), rows)
    half = DK_B // 2
    inv = ROPE_BASE ** (-jnp.arange(0, half, 2, dtype=F32) / half)
    ang = jnp.stack([row, col], axis=-1).astype(F32)[..., None] * inv
    return jnp.cos(ang), jnp.sin(ang)


def apply_axial_rope(x, cos, sin):
    shp = x.shape
    xr = x.astype(F32).reshape(*shp[:-1], 2, 2, DK_B // 4)
    x1, x2 = xr[..., 0, :], xr[..., 1, :]
    c = cos[None, :, None, None]
    s = sin[None, :, None, None]
    out = jnp.stack([x1 * c - x2 * s, x2 * c + x1 * s], axis=-2)
    return out.reshape(shp).astype(x.dtype)


def block_diff_attention(q, k, v, lam):
    bsz, n_q = q.shape[:2]
    n_blk = n_q // Q_BLOCK
    q_blocks = q.reshape(bsz, n_blk, Q_BLOCK, H_B, 2, DK_B).swapaxes(0, 1)
    scale = DK_B ** -0.5

    def one_block(qi):
        s = jnp.einsum('bqhrd,bkhrd->rbhqk', qi, k).astype(F32) * scale
        p = jax.nn.softmax(s, axis=-1)
        w = (p[0] - lam * p[1]).astype(v.dtype)
        return jnp.einsum('bhqk,bkhe->bqhe', w, v)

    o = lax.map(one_block, q_blocks)
    return o.swapaxes(0, 1).reshape(bsz, n_q, H_B, DV_B)


def mixer_diff(q, k, v, lam, subln_g, lam_init, rope, ctx_k, ctx_v):
    bsz, n_tok = q.shape[:2]
    if rope is not None:
        q = apply_axial_rope(q, *rope)
        k = apply_axial_rope(k, *rope)
    if ctx_k is not None:
        k = jnp.concatenate([k, ctx_k.reshape(bsz, -1, H_B, 2, DK_B)], axis=1)
        v = jnp.concatenate([v, ctx_v], axis=1)
    lam32 = lam.astype(F32)
    lam_full = (jnp.exp(jnp.sum(lam32[0] * lam32[1])) - jnp.exp(jnp.sum(lam32[2] * lam32[3]))
                + lam_init)
    o = block_diff_attention(q, k, v, lam_full)
    o = rmsnorm(o, subln_g, 1e-5) * (1.0 - lam_init)
    return o.reshape(bsz, n_tok, W_B)


def hyena_filters(n_tok, w1, b1, freq, w2, b2, w3):
    t = jnp.linspace(0.0, 1.0, n_tok, dtype=F32)[:, None]
    wpos = 2.0 * math.pi * jnp.arange(n_tok, dtype=F32)[:, None] / n_tok
    f = jnp.linspace(1e-4, FILT_BANDS - 1, FILT_BANDS, dtype=F32)
    z = jnp.concatenate([t, jnp.cos(wpos * f), -jnp.sin(wpos * f)], axis=-1)
    fr = freq.astype(F32)
    h = jnp.sin(fr * (z @ w1.astype(F32) + b1.astype(F32)))
    h = jnp.sin(fr * (h @ w2.astype(F32) + b2.astype(F32)))
    h = (h @ w3.astype(F32)).reshape(n_tok, HYENA_ORDER, 2, C_CH)
    max_decay = math.log(HYENA_TARGET) / HYENA_FAST_DECAY
    min_decay = math.log(HYENA_TARGET) / HYENA_SLOW_DECAY
    deltas = jnp.linspace(min_decay, max_decay, C_CH, dtype=F32)
    window = jnp.exp(-t * jnp.abs(deltas))
    return h * window[:, None, None, :]


def centred_filter_spectrum(h_fwd, h_bwd):
    hc = jnp.concatenate([h_fwd, jnp.zeros((1, h_fwd.shape[1]), F32), h_bwd[:0:-1]], axis=0)
    return jnp.fft.rfft(hc, axis=0)


def long_conv(u, spec, skip):
    n = u.shape[1]
    y = jnp.fft.irfft(jnp.fft.rfft(u, n=2 * n, axis=1) * spec[None], n=2 * n, axis=1)[:, :n]
    return y + u * skip


def mixer_hyena(xc, conv_w, w1, b1, freq, w2, b2, w3, skip):
    n_tok = xc.shape[1]
    xc = depthwise_conv_centred(xc, conv_w)
    x1, x2, v = jnp.split(xc, 3, axis=-1)
    h = hyena_filters(n_tok, w1, b1, freq, w2, b2, w3)
    z = v.astype(F32)
    for o, gate in enumerate((x1, x2)):
        spec = centred_filter_spectrum(h[:, o, 0], h[:, o, 1])
        z = gate.astype(F32) * long_conv(z, spec, skip[o].astype(F32))
    return z.astype(xc.dtype)


def moe(h, w_router, b_router, w_gu, b_gu, w_down, b_down):
    bsz, n_tok, d = h.shape
    t = h.reshape(-1, d)
    logits = (t @ w_router).astype(F32) + b_router.astype(F32)
    top_v, top_i = lax.top_k(logits, TOP_K)
    top_w = jax.nn.softmax(top_v, axis=-1)
    gates = jnp.sum(jax.nn.one_hot(top_i, N_EXPERTS, dtype=F32) * top_w[..., None], axis=-2)
    gu = jnp.einsum('td,edf->etf', t, w_gu) + b_gu[:, None, :]
    gate = jnp.minimum(gu[..., :D_FF], SWIGLU_LIMIT)
    up = jnp.clip(gu[..., D_FF:], -SWIGLU_LIMIT, SWIGLU_LIMIT)
    act = (up + 1.0) * gate * jax.nn.sigmoid(SWIGLU_ALPHA * gate)
    act = act * gates.T[..., None].astype(act.dtype)
    out = jnp.einsum('etf,efd->td', act, w_down) + gates.astype(t.dtype) @ b_down
    return out.reshape(bsz, n_tok, d)


def trunk_layer(x, mod, lidx, p, rope=None, cache=None):
    bsz, n_tok, _ = x.shape
    shift1, scale1, gate1, shift2, scale2, gate2 = jnp.split(mod, N_MOD, axis=-1)
    h = rmsnorm(x, p['norm1_g'][lidx]) * (1.0 + scale1) + shift1
    proj = h @ p['w_in'][lidx]
    qa, ka, va, za, aa, ba, qb, kb, vb, xc, gates = split_columns(proj)
    ctx_k = ctx_v = s0_f = s0_b = None
    if cache is not None:
        ctx_k, ctx_v, s0_f, s0_b = cache
    o_a, s_f, s_b = mixer_gdn(qa, ka, va, za, aa, ba, p['conv_a'][lidx], p['a_log'][lidx],
                              p['dt_bias'][lidx], p['onorm_a'][lidx], s0_f, s0_b)
    lam_init = 0.8 - 0.6 * math.exp(-0.3 * lidx)
    k_b = kb.reshape(bsz, n_tok, H_B, 2, DK_B)
    v_b = vb.reshape(bsz, n_tok, H_B, DV_B)
    o_b = mixer_diff(qb.reshape(bsz, n_tok, H_B, 2, DK_B), k_b, v_b, p['lam'][lidx],
                     p['subln_b'][lidx], lam_init, rope, ctx_k, ctx_v)
    o_c = mixer_hyena(xc, p['conv_c'][lidx], p['filt_w1'][lidx], p['filt_b1'][lidx],
                      p['filt_freq'][lidx], p['filt_w2'][lidx], p['filt_b2'][lidx],
                      p['filt_w3'][lidx], p['filt_skip'][lidx])
    g_a, g_b, g_c = jnp.split(jax.nn.sigmoid(gates), 3, axis=-1)
    merged = (g_a * (o_a @ p['w_br_a'][lidx]) + g_b * (o_b @ p['w_br_b'][lidx])
              + g_c * (o_c @ p['w_br_c'][lidx]))
    x = x + gate1 * (merged @ p['w_out'][lidx])
    h2 = rmsnorm(x, p['norm2_g'][lidx]) * (1.0 + scale2) + shift2
    x = x + gate2 * moe(h2, p['w_router'][lidx], p['b_router'][lidx], p['w_gu'][lidx],
                        p['b_gu'][lidx], p['w_down'][lidx], p['b_down'][lidx])
    ctx_out = (k_b.reshape(bsz, n_tok, H_B, 2 * DK_B), v_b,
               s_f.astype(x.dtype), s_b.astype(x.dtype))
    return x, ctx_out


def setup_inputs(seed: int = 0) -> dict:
    key = jax.random.key(seed)
    keys = iter(jax.random.split(key, 48))

    def nrm(shape, std):
        return jax.random.normal(next(keys), shape, F32) * std

    def gain(shape):
        return 1.0 + nrm(shape, 0.1)

    dt = jnp.exp(jax.random.uniform(next(keys), (DEPTH, 2, H_A), F32,
                                    minval=math.log(1e-3), maxval=math.log(0.1)))
    return {
        'x_prompt': nrm((BATCH, SEQ, D_MODEL), 1.0),
        'x_sample': nrm((DEC_BATCH, DEC_SEQ, D_MODEL), 1.0),
        'cache_k': nrm((DEC_BATCH, DEPTH, PAST_LEN, H_B, 2 * DK_B), 1.0),
        'cache_v': nrm((DEC_BATCH, DEPTH, PAST_LEN, H_B, DV_B), 1.0),
        'state_fwd': nrm((DEC_BATCH, DEPTH, H_A, DK_A, DV_A), 0.3),
        'state_bwd': nrm((DEC_BATCH, DEPTH, H_A, DK_A, DV_A), 0.3),
        'c': nrm((DEC_BATCH, D_MODEL), 1.0),
        'c_ctx': nrm((D_MODEL,), 1.0),
        'norm1_g': gain((DEPTH, D_MODEL)),
        'norm2_g': gain((DEPTH, D_MODEL)),
        'final_g': gain((D_MODEL,)),
        'w_mod': nrm((DEPTH, D_MODEL, N_MOD * D_MODEL), 0.5 * D_MODEL ** -0.5),
        'b_mod': nrm((DEPTH, N_MOD * D_MODEL), 0.01),
        'w_in': nrm((DEPTH, D_MODEL, N_IN), D_MODEL ** -0.5),
        'conv_a': nrm((DEPTH, SHORT_CONV, 3 * W_A), SHORT_CONV ** -0.5),
        'a_log': jnp.log(jax.random.uniform(next(keys), (DEPTH, 2, H_A), F32, minval=1.0, maxval=16.0)),
        'dt_bias': dt + jnp.log(-jnp.expm1(-dt)),
        'onorm_a': gain((DEPTH, DV_A)),
        'lam': nrm((DEPTH, 4, DK_B), 0.1),
        'subln_b': gain((DEPTH, DV_B)),
        'conv_c': nrm((DEPTH, SHORT_CONV, (HYENA_ORDER + 1) * C_CH), SHORT_CONV ** -0.5),
        'filt_w1': nrm((DEPTH, FILT_EMB, FILT_HIDDEN), FILT_EMB ** -0.5),
        'filt_b1': nrm((DEPTH, FILT_HIDDEN), 0.1),
        'filt_freq': gain((DEPTH, FILT_HIDDEN)),
        'filt_w2': nrm((DEPTH, FILT_HIDDEN, FILT_HIDDEN), FILT_HIDDEN ** -0.5),
        'filt_b2': nrm((DEPTH, FILT_HIDDEN), 0.1),
        'filt_w3': nrm((DEPTH, FILT_HIDDEN, HYENA_ORDER * 2 * C_CH), FILT_OUT_STD),
        'filt_skip': nrm((DEPTH, HYENA_ORDER, C_CH), 0.5),
        'w_br_a': nrm((DEPTH, W_A, D_MODEL), W_A ** -0.5),
        'w_br_b': nrm((DEPTH, W_B, D_MODEL), W_B ** -0.5),
        'w_br_c': nrm((DEPTH, C_CH, D_MODEL), C_CH ** -0.5),
        'w_out': nrm((DEPTH, D_MODEL, D_MODEL), D_MODEL ** -0.5),
        'w_router': nrm((DEPTH, D_MODEL, N_EXPERTS), D_MODEL ** -0.5),
        'b_router': nrm((DEPTH, N_EXPERTS), 0.01),
        'w_gu': nrm((DEPTH, N_EXPERTS, D_MODEL, 2 * D_FF), D_MODEL ** -0.5),
        'b_gu': nrm((DEPTH, N_EXPERTS, 2 * D_FF), 0.01),
        'w_down': nrm((DEPTH, N_EXPERTS, D_FF, D_MODEL), D_FF ** -0.5),
        'b_down': nrm((DEPTH, N_EXPERTS, D_MODEL), 0.01),
    }


def reference(x_prompt, x_sample, cache_k, cache_v, state_fwd, state_bwd, c, c_ctx,
              norm1_g, norm2_g, final_g, w_mod, b_mod, w_in, conv_a, a_log, dt_bias, onorm_a,
              lam, subln_b, conv_c, filt_w1, filt_b1, filt_freq, filt_w2, filt_b2, filt_w3,
              filt_skip, w_br_a, w_br_b, w_br_c, w_out, w_router, b_router, w_gu, b_gu,
              w_down, b_down):
    p = dict(norm1_g=norm1_g, norm2_g=norm2_g, w_in=w_in, conv_a=conv_a, a_log=a_log,
             dt_bias=dt_bias, onorm_a=onorm_a, lam=lam, subln_b=subln_b, conv_c=conv_c,
             filt_w1=filt_w1, filt_b1=filt_b1, filt_freq=filt_freq, filt_w2=filt_w2,
             filt_b2=filt_b2, filt_w3=filt_w3, filt_skip=filt_skip, w_br_a=w_br_a,
             w_br_b=w_br_b, w_br_c=w_br_c, w_out=w_out, w_router=w_router, b_router=b_router,
             w_gu=w_gu, b_gu=b_gu, w_down=w_down, b_down=b_down)

    xp = x_prompt
    ks, vs, sfs, sbs = [], [], [], []
    for l in range(DEPTH):
        mod = (jax.nn.silu(c_ctx) @ w_mod[l] + b_mod[l])[None, None, :]
        xp, (k_l, v_l, sf_l, sb_l) = trunk_layer(xp, mod, l, p)
        ks.append(k_l)
        vs.append(v_l)
        sfs.append(sf_l)
        sbs.append(sb_l)
    y_prompt = rmsnorm(xp, final_g)

    rope = axial_rope_tables(x_sample.shape[1])
    xs = x_sample
    for l in range(DEPTH):
        mod = (jax.nn.silu(c) @ w_mod[l] + b_mod[l])[:, None, :]
        xs, _ = trunk_layer(xs, mod, l, p, rope=rope,
                            cache=(cache_k[:, l], cache_v[:, l], state_fwd[:, l], state_bwd[:, l]))
    y_sample = rmsnorm(xs, final_g)

    new_cache_k = jnp.stack(ks, axis=1)
    new_cache_v = jnp.stack(vs, axis=1)
    new_state_fwd = jnp.stack(sfs, axis=1)
    new_state_bwd = jnp.stack(sbs, axis=1)
    return (y_prompt, y_sample, new_cache_k, new_cache_v, new_state_fwd, new_state_bwd)
```

```python
import functools
import math

import jax
import jax.numpy as jnp
from jax import lax
from jax.experimental import pallas as pl
from jax.experimental.pallas import tpu as pltpu

F32 = jnp.float32
BF16 = jnp.bfloat16

D_MODEL = 4096
DEPTH = 2
GRID_W = 64
H_A = 12
DK_A = 128
W_A = H_A * DK_A
SHORT_CONV = 3
H_B = 6
DK_B = 128
DV_B = 2 * DK_B
W_B = H_B * DV_B
ROPE_BASE = 10000.0
C_CH = 1024
HYENA_ORDER = 2
FILT_BANDS = 16
FILT_EMB = 1 + 2 * FILT_BANDS
FILT_HIDDEN = 64
HYENA_FAST_DECAY = 0.3
HYENA_SLOW_DECAY = 1.5
HYENA_TARGET = 1e-2
N_EXPERTS = 32
TOP_K = 4
D_FF = 1024
SWIGLU_LIMIT = 7.0
SWIGLU_ALPHA = 1.702
N_MOD = 6
EPS = 1e-6

OFF_AB = 4 * W_A
OFF_REST = OFF_AB + 4 * H_A
R_QB, R_KB, R_VB = 0, W_B, 2 * W_B
R_XC = 3 * W_B
R_GATES = R_XC + 3 * C_CH
N_REST = R_GATES + 3 * D_MODEL

LANES = 128
VMEM_LIMIT = 56 * 1024 * 1024
GDN_CHUNK = 256
ATT_QBLOCK = 256
MOE_TM = 512
MOE_TF = 256
COMBINE_TM = 128


def _cparams(**kw):
    return pltpu.CompilerParams(vmem_limit_bytes=VMEM_LIMIT, **kw)


def _bf(x):
    return x.astype(BF16)


def _dot(a, b):
    return jnp.dot(_bf(a), _bf(b), preferred_element_type=F32)


def _dot_nt(a, b):
    return lax.dot_general(_bf(a), _bf(b), (((1,), (1,)), ((), ())), preferred_element_type=F32)


def _dot_tn(a, b):
    return lax.dot_general(_bf(a), _bf(b), (((0,), (0,)), ((), ())), preferred_element_type=F32)


def _split2(x):
    hi = _bf(x)
    lo = _bf(x - hi.astype(F32))
    return hi, lo


def _split3(x):
    hi = _bf(x)
    r = x - hi.astype(F32)
    mid = _bf(r)
    lo = _bf(r - mid.astype(F32))
    return hi, mid, lo


def _dot_hl(a, b):
    ah, al = _split2(a)
    bh, bl = _split2(b)
    d = functools.partial(jnp.dot, preferred_element_type=F32)
    return d(ah, bh) + (d(ah, bl) + d(al, bh))


def _dot_exact_lhs(m01, x):
    m = _bf(m01)
    h, mi, lo = _split3(x)
    d = functools.partial(jnp.dot, preferred_element_type=F32)
    return d(m, h) + (d(m, mi) + d(m, lo))


def _sigmoid(x):
    return 1.0 / (1.0 + jnp.exp(-x))


def _silu(x):
    return x * _sigmoid(x)


def _softplus(x):
    return jnp.maximum(x, 0.0) + jnp.log(1.0 + jnp.exp(-jnp.abs(x)))


def _mod_row(row_start, n_ctx_rows, rows_per_latent):
    return jnp.where(row_start < n_ctx_rows, 0, 1 + (row_start - n_ctx_rows) // rows_per_latent)


def _mm_kernel(x_ref, w_ref, o_ref, acc_ref, *, nk):
    k = pl.program_id(2)

    @pl.when(k == 0)
    def _():
        acc_ref[...] = jnp.zeros_like(acc_ref)

    acc_ref[...] += _dot(x_ref[...], w_ref[...])

    @pl.when(k == nk - 1)
    def _():
        o_ref[...] = acc_ref[...].astype(o_ref.dtype)


def _mm_bias_kernel(x_ref, w_ref, b_ref, o_ref, acc_ref, *, nk):
    k = pl.program_id(2)

    @pl.when(k == 0)
    def _():
        acc_ref[...] = jnp.zeros_like(acc_ref)

    acc_ref[...] += _dot(x_ref[...], w_ref[...])

    @pl.when(k == nk - 1)
    def _():
        o_ref[...] = (acc_ref[...] + b_ref[...]).astype(o_ref.dtype)


def _matmul(x, w, *, n, col0=0, bias=None, tm, tn, tk, out_dtype=F32, name):
    m, kdim = x.shape
    assert m % tm == 0 and n % tn == 0 and kdim % tk == 0 and col0 % tn == 0
    nk = kdim // tk
    cb = col0 // tn
    in_specs = [pl.BlockSpec((tm, tk), lambda i, j, k: (i, k)),
                pl.BlockSpec((tk, tn), lambda i, j, k: (k, j + cb))]
    args = [x, w]
    if bias is None:
        body = functools.partial(_mm_kernel, nk=nk)
    else:
        body = functools.partial(_mm_bias_kernel, nk=nk)
        in_specs.append(pl.BlockSpec((1, tn), lambda i, j, k: (0, j)))
        args.append(bias.reshape(1, n))
    return pl.pallas_call(
        body,
        out_shape=jax.ShapeDtypeStruct((m, n), out_dtype),
        grid=(m // tm, n // tn, nk),
        in_specs=in_specs,
        out_specs=pl.BlockSpec((tm, tn), lambda i, j, k: (i, j)),
        scratch_shapes=[pltpu.VMEM((tm, tn), F32)],
        compiler_params=_cparams(),
        name=name,
    )(*args)


def _adaln_kernel(x_ref, g_ref, scale_ref, shift_ref, o_ref):
    x = x_ref[...]
    y = x * lax.rsqrt(jnp.mean(x * x, axis=-1, keepdims=True) + EPS)
    o_ref[...] = (y * g_ref[...] * (1.0 + scale_ref[...]) + shift_ref[...]).astype(o_ref.dtype)


def _adaln(x, g, mod4, shift_idx, scale_idx, *, n_ctx_rows, rows_per_latent, tm=256):
    t, d = x.shape
    row = lambda i: _mod_row(i * tm, n_ctx_rows, rows_per_latent)
    return pl.pallas_call(
        _adaln_kernel,
        out_shape=jax.ShapeDtypeStruct((t, d), BF16),
        grid=(t // tm,),
        in_specs=[pl.BlockSpec((tm, d), lambda i: (i, 0)),
                  pl.BlockSpec((1, d), lambda i: (0, 0)),
                  pl.BlockSpec((None, None, 1, d), lambda i: (row(i), scale_idx, 0, 0)),
                  pl.BlockSpec((None, None, 1, d), lambda i: (row(i), shift_idx, 0, 0))],
        out_specs=pl.BlockSpec((tm, d), lambda i: (i, 0)),
        compiler_params=_cparams(),
        name="adaln",
    )(x, g.reshape(1, d), mod4, mod4)


def _gdn_gate_kernel(ab_ref, alog_ref, dtb_ref, o_ref):
    ab = ab_ref[...]
    lane = lax.broadcasted_iota(jnp.int32, ab.shape, 1)
    g = -jnp.exp(alog_ref[...]) * _softplus(ab + dtb_ref[...])
    o_ref[...] = jnp.where(lane < 2 * H_A, g, _sigmoid(ab))


def _gdn_gates(ab, a_log, dt_bias, tm=1024):
    t = ab.shape[0]
    pad = lambda v: jnp.pad(v.reshape(1, 2 * H_A), ((0, 0), (0, LANES - 2 * H_A)))
    return pl.pallas_call(
        _gdn_gate_kernel,
        out_shape=jax.ShapeDtypeStruct((t, LANES), F32),
        grid=(t // tm,),
        in_specs=[pl.BlockSpec((tm, LANES), lambda i: (i, 0)),
                  pl.BlockSpec((1, LANES), lambda i: (0, 0)),
                  pl.BlockSpec((1, LANES), lambda i: (0, 0))],
        out_specs=pl.BlockSpec((tm, LANES), lambda i: (i, 0)),
        compiler_params=_cparams(),
        name="gdn_gates",
    )(ab, pad(a_log), pad(dt_bias))


def _unit_tri_inverse(a, rows, cols, dot):
    n = a.shape[0]
    eye = (rows == cols).astype(F32)
    same = lambda s: (rows // s) == (cols // s)
    d1 = jnp.where(same(16), a, 0.0)
    t = eye - d1
    dp = d1
    for _ in range(3):
        dp = dot(dp, dp)
        t = t + dot(t, dp)
    s = 16
    while s < n:
        off = jnp.where(same(2 * s) & jnp.logical_not(same(s)), a, 0.0)
        t = t - dot(t, dot(off, t))
        s *= 2
    return t


def _gdn_kernel(q_ref, k_ref, v_ref, z_ref, gate_ref, cw_ref, og_ref, s0f_ref, s0b_ref,
                o_ref, sf_ref, sb_ref, qs, ks, vs, of_s, ob_s, *, seq_len, chunk):
    h = pl.program_id(1)
    n_chunks = seq_len // chunk

    pos = lax.broadcasted_iota(jnp.int32, (seq_len, DK_A), 0)

    def conv_silu(x_ref, w):
        x = x_ref[...]
        prev = jnp.where(pos == 0, 0.0, pltpu.roll(x, 1, 0))
        nxt = jnp.where(pos == seq_len - 1, 0.0, pltpu.roll(x, seq_len - 1, 0))
        return _silu(prev * w[0:1, :] + x * w[1:2, :] + nxt * w[2:3, :])

    def l2n(x):
        return x * lax.rsqrt(jnp.sum(x * x, axis=-1, keepdims=True) + 1e-6)

    qs[...] = l2n(conv_silu(q_ref, cw_ref[0])) * (DK_A ** -0.5)
    ks[...] = l2n(conv_silu(k_ref, cw_ref[1]))
    vs[...] = conv_silu(v_ref, cw_ref[2])

    rows = lax.broadcasted_iota(jnp.int32, (chunk, chunk), 0)
    cols = lax.broadcasted_iota(jnp.int32, (chunk, chunk), 1)
    lane = lax.broadcasted_iota(jnp.int32, (chunk, LANES), 1)
    ones_cc = jnp.ones((chunk, chunk), F32)

    def one_chunk(c, state, direction, out_ref):
        r0 = pl.multiple_of(c * chunk, chunk)
        q = qs[pl.ds(r0, chunk), :]
        k = ks[pl.ds(r0, chunk), :]
        v = vs[pl.ds(r0, chunk), :]
        gates = gate_ref[pl.ds(r0, chunk), :]
        g = jnp.sum(jnp.where(lane == direction * H_A + h, gates, 0.0), axis=-1, keepdims=True)
        beta = jnp.sum(jnp.where(lane == (2 + direction) * H_A + h, gates, 0.0), axis=-1, keepdims=True)
        if direction == 0:
            incl, incl_t = cols <= rows, rows <= cols
            strict = cols < rows
        else:
            incl, incl_t = cols >= rows, rows >= cols
            strict = cols > rows
        g_rows = jnp.broadcast_to(g, (chunk, chunk))
        gc = _dot_exact_lhs(incl.astype(F32), g_rows)
        gr = _dot_exact_lhs(ones_cc, jnp.where(incl_t, g_rows, 0.0))
        decay = jnp.where(incl, jnp.exp(jnp.where(incl, gc - gr, 0.0)), 0.0)
        gcum = gc[:, 0:LANES]
        g_last = jnp.sum(jnp.broadcast_to(g, (chunk, LANES)), axis=0, keepdims=True)
        kk = _dot_nt(k, k)
        a = jnp.where(strict, beta * kk * decay, 0.0)
        t = _unit_tri_inverse(a, rows, cols, _dot_hl)
        kb = k * beta
        rhs = jnp.concatenate([v * beta, kb * jnp.exp(gcum)], axis=-1)
        uw = _dot(t, rhs)
        u, w = uw[:, :DK_A], uw[:, DK_A:]
        qk = jnp.where(incl, _dot_nt(q, k) * decay, 0.0)
        qg = q * jnp.exp(gcum)
        ws = _dot(jnp.concatenate([w, qg], axis=0), state)
        v_new = u - ws[:chunk]
        out_ref[pl.ds(r0, chunk), :] = ws[chunk:] + _dot(qk, v_new)
        kd = k * jnp.exp(g_last - gcum)
        return state * jnp.exp(g_last[:, 0:1]) + _dot_tn(kd, v_new)

    def body(i, carry):
        s_f, s_b = carry
        s_f = one_chunk(i, s_f, 0, of_s)
        s_b = one_chunk(n_chunks - 1 - i, s_b, 1, ob_s)
        return s_f, s_b

    s_f, s_b = lax.fori_loop(0, n_chunks, body, (s0f_ref[...], s0b_ref[...]))
    sf_ref[...] = s_f
    sb_ref[...] = s_b
    o = of_s[...] + ob_s[...]
    o = o * lax.rsqrt(jnp.mean(o * o, axis=-1, keepdims=True) + EPS) * og_ref[...]
    o_ref[...] = (o * _silu(z_ref[...])).astype(o_ref.dtype)


def _gdn(proj_a, gates, conv_w, onorm_g, s0f, s0b, *, n_seq, seq_len, row0):
    assert row0 % seq_len == 0 and seq_len % GDN_CHUNK == 0
    rb = row0 // seq_len
    cw = conv_w.reshape(SHORT_CONV, 3, H_A, DK_A).transpose(2, 1, 0, 3)
    tok = lambda part: pl.BlockSpec((seq_len, DK_A), lambda b, h: (b + rb, part * H_A + h))
    st = pl.BlockSpec((None, None, DK_A, DK_A), lambda b, h: (b, h, 0, 0))
    body = functools.partial(_gdn_kernel, seq_len=seq_len, chunk=GDN_CHUNK)
    return pl.pallas_call(
        body,
        out_shape=(jax.ShapeDtypeStruct((n_seq * seq_len, W_A), BF16),
                   jax.ShapeDtypeStruct((n_seq, H_A, DK_A, DK_A), F32),
                   jax.ShapeDtypeStruct((n_seq, H_A, DK_A, DK_A), F32)),
        grid=(n_seq, H_A),
        in_specs=[tok(0), tok(1), tok(2), tok(3),
                  pl.BlockSpec((seq_len, LANES), lambda b, h: (b + rb, 0)),
                  pl.BlockSpec((None, 3, SHORT_CONV, DK_A), lambda b, h: (h, 0, 0, 0)),
                  pl.BlockSpec((1, DK_A), lambda b, h: (0, 0)),
                  st, st],
        out_specs=(pl.BlockSpec((seq_len, DK_A), lambda b, h: (b, h)), st, st),
        scratch_shapes=[pltpu.VMEM((seq_len, DK_A), F32) for _ in range(5)],
        compiler_params=_cparams(),
        name=f"gdn_L{seq_len}",
    )(proj_a, proj_a, proj_a, proj_a, gates, cw, onorm_g.reshape(1, DK_A), s0f, s0b)


def _rope(x, cos, sin_signed, lane):
    rot = jnp.where((lane % 64) < 32, pltpu.roll(x, LANES - 32, 1), pltpu.roll(x, 32, 1))
    return x * cos + rot * sin_signed


def _attn_kernel(*refs, seq_len, qblock, use_rope, n_cache, lam_init):
    it = iter(refs)
    q_ref, k_ref, v_ref, lam_ref, g_ref = next(it), next(it), next(it), next(it), next(it)
    cos_ref = sin_ref = ck_ref = cv_ref = None
    if use_rope:
        cos_ref, sin_ref = next(it), next(it)
    if n_cache:
        ck_ref, cv_ref = next(it), next(it)
    o_ref = next(it)
    ks = next(it)

    lam = lam_ref[...]
    lam_full = (jnp.exp(jnp.sum(lam[0:1] * lam[1:2], axis=-1, keepdims=True))
                - jnp.exp(jnp.sum(lam[2:3] * lam[3:4], axis=-1, keepdims=True)) + lam_init)
    scale = DK_B ** -0.5
    lane = lax.broadcasted_iota(jnp.int32, (seq_len, DK_B), 1) if use_rope else None
    lane_q = lax.broadcasted_iota(jnp.int32, (qblock, DK_B), 1) if use_rope else None
    for r in range(2):
        kr = k_ref[:, r * DK_B:(r + 1) * DK_B]
        if use_rope:
            kr = _rope(kr, cos_ref[...], sin_ref[...], lane)
        ks[r] = _bf(kr)
    v = _bf(v_ref[...])
    for qb in range(seq_len // qblock):
        sl = slice(qb * qblock, (qb + 1) * qblock)
        probs = []
        for r in range(2):
            qr = q_ref[sl, r * DK_B:(r + 1) * DK_B]
            if use_rope:
                qr = _rope(qr, cos_ref[sl, :], sin_ref[sl, :], lane_q)
            s = _dot_nt(qr, ks[r]) * scale
            m = jnp.max(s, axis=-1, keepdims=True)
            if n_cache:
                sc = _dot_nt(qr, ck_ref[:, r * DK_B:(r + 1) * DK_B]) * scale
                m = jnp.maximum(m, jnp.max(sc, axis=-1, keepdims=True))
                ec = jnp.exp(sc - m)
            e = jnp.exp(s - m)
            den = jnp.sum(e, axis=-1, keepdims=True)
            if n_cache:
                den = den + jnp.sum(ec, axis=-1, keepdims=True)
                probs.append((e / den, ec / den))
            else:
                probs.append((e / den, None))
        o = _dot(probs[0][0] - lam_full * probs[1][0], v)
        if n_cache:
            o = o + _dot(probs[0][1] - lam_full * probs[1][1], cv_ref[...])
        o = o * lax.rsqrt(jnp.mean(o * o, axis=-1, keepdims=True) + 1e-5) * g_ref[...]
        o_ref[sl, :] = (o * (1.0 - lam_init)).astype(o_ref.dtype)


def _attention(proj_r, lam, subln_g, lam_init, *, n_seq, seq_len, row0, rope=None, cache=None):
    assert row0 % seq_len == 0
    rb = row0 // seq_len
    nh = H_B
    blk = lambda part: pl.BlockSpec((seq_len, DV_B), lambda b, h: (b + rb, part * nh + h))
    in_specs = [blk(0), blk(1), blk(2),
                pl.BlockSpec((4, DK_B), lambda b, h: (0, 0)),
                pl.BlockSpec((1, DV_B), lambda b, h: (0, 0))]
    args = [proj_r, proj_r, proj_r, lam, subln_g.reshape(1, DV_B)]
    if rope is not None:
        in_specs += [pl.BlockSpec((seq_len, DK_B), lambda b, h: (0, 0))] * 2
        args += list(rope)
    n_cache = 0
    if cache is not None:
        ck, cv = cache
        n_cache = ck.shape[1]
        in_specs += [pl.BlockSpec((None, n_cache, DV_B), lambda b, h: (b, 0, h))] * 2
        args += [ck, cv]
    body = functools.partial(_attn_kernel, seq_len=seq_len, qblock=min(ATT_QBLOCK, seq_len),
                             use_rope=rope is not None, n_cache=n_cache, lam_init=lam_init)
    return pl.pallas_call(
        body,
        out_shape=jax.ShapeDtypeStruct((n_seq * seq_len, W_B), BF16),
        grid=(n_seq, nh),
        in_specs=in_specs,
        out_specs=pl.BlockSpec((seq_len, DV_B), lambda b, h: (b, h)),
        scratch_shapes=[pltpu.VMEM((2, seq_len, DK_B), BF16)],
        compiler_params=_cparams(),
        name=f"diff_attn_L{seq_len}",
    )(*args)


def _rope_tables(n_tok):
    rows = n_tok // GRID_W
    row = jnp.repeat(jnp.arange(rows), GRID_W)
    col = jnp.tile(jnp.arange(GRID_W), rows)
    half = DK_B // 2
    inv = ROPE_BASE ** (-jnp.arange(0, half, 2, dtype=F32) / half)
    ang = jnp.stack([row, col], axis=-1).astype(F32)[..., None] * inv
    cos, sin = jnp.cos(ang), jnp.sin(ang)
    cos_t = jnp.concatenate([cos, cos], axis=-1).reshape(n_tok, DK_B)
    sin_t = jnp.concatenate([-sin, sin], axis=-1).reshape(n_tok, DK_B)
    return cos_t, sin_t


def _dft_tables(n):
    f = jnp.arange(n, dtype=jnp.int32)[:, None]
    t = jnp.arange(n, dtype=jnp.int32)[None, :]
    ang = ((f * t) % (2 * n)).astype(F32) * (math.pi / n)
    cos, sin = jnp.cos(ang), jnp.sin(ang)
    nyq = jnp.where(t % 2 == 0, 1.0, -1.0).astype(F32)
    fwd_im = jnp.where(f == 0, nyq, -sin)
    fwd = jnp.concatenate([cos, fwd_im], axis=0)
    wgt = jnp.where(f == 0, 1.0, 2.0).astype(F32) / (2 * n)
    inv_re = (wgt * cos).T
    inv_im = jnp.where(f == 0, nyq / (2 * n), -wgt * sin).T
    inv = jnp.concatenate([inv_re, inv_im], axis=1)
    return fwd, inv


def _spec_mul(u, s, n, row):
    ur, ui, sr, si = u[:n], u[n:], s[:n], s[n:]
    first = row == 0
    yr = ur * sr - jnp.where(first, 0.0, ui * si)
    yi = jnp.where(first, ui * si, ur * si + ui * sr)
    return yr, yi


def _hyena_filter_kernel(z_ref, w1_ref, b1_ref, fr_ref, w2_ref, b2_ref, w3_ref, win_ref,
                         fh_ref, fl_ref, o_ref, *, n):
    fr = fr_ref[...]
    h = jnp.sin(fr * (_dot_hl(z_ref[...], w1_ref[...]) + b1_ref[...]))
    h = jnp.sin(fr * (_dot_hl(h, w2_ref[...]) + b2_ref[...]))
    win = win_ref[...]
    row = lax.broadcasted_iota(jnp.int32, win.shape, 0)
    fwd_hi, fwd_lo = fh_ref[...], fl_ref[...]

    def dft(x):
        xh, xl = _split2(x)
        d = functools.partial(jnp.dot, preferred_element_type=F32)
        return d(fwd_hi, xh) + (d(fwd_hi, xl) + d(fwd_lo, xh))

    for o in range(HYENA_ORDER):
        hf = _dot_hl(h, w3_ref[2 * o]) * win
        hb = jnp.where(row == 0, 0.0, _dot_hl(h, w3_ref[2 * o + 1]) * win)
        p, q = dft(hf), dft(hb)
        o_ref[o, :n, :] = p[:n] + q[:n]
        o_ref[o, n:, :] = jnp.where(row == 0, p[n:] + q[n:], p[n:] - q[n:])


def _hyena_filters(n, w1, b1, freq, w2, b2, w3, fwd_hi, fwd_lo, tc=256):
    t = jnp.linspace(0.0, 1.0, n, dtype=F32)[:, None]
    wpos = 2.0 * math.pi * jnp.arange(n, dtype=F32)[:, None] / n
    f = jnp.linspace(1e-4, FILT_BANDS - 1, FILT_BANDS, dtype=F32)
    z = jnp.concatenate([t, jnp.cos(wpos * f), -jnp.sin(wpos * f)], axis=-1)
    z = jnp.pad(z, ((0, 0), (0, LANES - FILT_EMB)))
    w1p = jnp.pad(w1, ((0, LANES - FILT_EMB), (0, 0)))
    max_decay = math.log(HYENA_TARGET) / HYENA_FAST_DECAY
    min_decay = math.log(HYENA_TARGET) / HYENA_SLOW_DECAY
    deltas = jnp.linspace(min_decay, max_decay, C_CH, dtype=F32)
    window = jnp.exp(-t * jnp.abs(deltas))
    w3r = w3.reshape(FILT_HIDDEN, 2 * HYENA_ORDER, C_CH).transpose(1, 0, 2)
    full = lambda shape: pl.BlockSpec(shape, lambda j: (0,) * len(shape))
    return pl.pallas_call(
        functools.partial(_hyena_filter_kernel, n=n),
        out_shape=jax.ShapeDtypeStruct((HYENA_ORDER, 2 * n, C_CH), F32),
        grid=(C_CH // tc,),
        in_specs=[full((n, LANES)), full((LANES, FILT_HIDDEN)), full((1, FILT_HIDDEN)),
                  full((1, FILT_HIDDEN)), full((FILT_HIDDEN, FILT_HIDDEN)), full((1, FILT_HIDDEN)),
                  pl.BlockSpec((2 * HYENA_ORDER, FILT_HIDDEN, tc), lambda j: (0, 0, j)),
                  pl.BlockSpec((n, tc), lambda j: (0, j)),
                  full((2 * n, n)), full((2 * n, n))],
        out_specs=pl.BlockSpec((HYENA_ORDER, 2 * n, tc), lambda j: (0, 0, j)),
        compiler_params=_cparams(),
        name=f"hyena_filter_L{n}",
    )(z, w1p, b1.reshape(1, -1), freq.reshape(1, -1), w2, b2.reshape(1, -1), w3r, window,
      fwd_hi, fwd_lo)


def _hyena_kernel(x1_ref, x2_ref, v_ref, cw_ref, spec_ref, skip_ref, fwd_ref, inv_ref, o_ref, *, n):
    shape = v_ref.shape
    row = lax.broadcasted_iota(jnp.int32, shape, 0)

    def conv3(x_ref, p):
        x = x_ref[...]
        prev = jnp.where(row == 0, 0.0, pltpu.roll(x, 1, 0))
        nxt = jnp.where(row == n - 1, 0.0, pltpu.roll(x, n - 1, 0))
        return prev * cw_ref[p, 0:1, :] + x * cw_ref[p, 1:2, :] + nxt * cw_ref[p, 2:3, :]

    z = conv3(v_ref, 2)
    fwd, inv = fwd_ref[...], inv_ref[...]
    for o, gate_ref in enumerate((x1_ref, x2_ref)):
        u = jnp.dot(fwd, _bf(z), preferred_element_type=F32)
        yr, yi = _spec_mul(u, spec_ref[o], n, row)
        y = jnp.dot(inv, _bf(jnp.concatenate([yr, yi], axis=0)), preferred_element_type=F32)
        z = conv3(gate_ref, o) * (y + z * skip_ref[o:o + 1, :])
    o_ref[...] = z.astype(o_ref.dtype)


def _hyena(proj_r, conv_w, spec, skip, fwd, inv, *, n_seq, seq_len, row0, tc=256):
    assert row0 % seq_len == 0
    rb = row0 // seq_len
    xb = lambda part: pl.BlockSpec((seq_len, tc), lambda b, j: (b + rb, (R_XC + part * C_CH) // tc + j))
    cw = conv_w.reshape(SHORT_CONV, 3, C_CH).transpose(1, 0, 2)
    return pl.pallas_call(
        functools.partial(_hyena_kernel, n=seq_len),
        out_shape=jax.ShapeDtypeStruct((n_seq * seq_len, C_CH), BF16),
        grid=(n_seq, C_CH // tc),
        in_specs=[xb(0), xb(1), xb(2),
                  pl.BlockSpec((3, SHORT_CONV, tc), lambda b, j: (0, 0, j)),
                  pl.BlockSpec((HYENA_ORDER, 2 * seq_len, tc), lambda b, j: (0, 0, j)),
                  pl.BlockSpec((HYENA_ORDER, tc), lambda b, j: (0, j)),
                  pl.BlockSpec((2 * seq_len, seq_len), lambda b, j: (0, 0)),
                  pl.BlockSpec((seq_len, 2 * seq_len), lambda b, j: (0, 0))],
        out_specs=pl.BlockSpec((seq_len, tc), lambda b, j: (b, j)),
        compiler_params=_cparams(),
        name=f"hyena_L{seq_len}",
    )(proj_r, proj_r, proj_r, cw, spec, skip, fwd, inv)


def _merge_kernel(oa_ref, ob_ref, oc_ref, wa_ref, wb_ref, wc_ref, ga_ref, gb_ref, gc_ref, o_ref):
    acc = _sigmoid(ga_ref[...]) * _dot(oa_ref[...], wa_ref[...])
    acc += _sigmoid(gb_ref[...]) * _dot(ob_ref[...], wb_ref[...])
    acc += _sigmoid(gc_ref[...]) * _dot(oc_ref[...], wc_ref[...])
    o_ref[...] = acc.astype(o_ref.dtype)


def _merge(o_a, o_b, o_c, w_a, w_b, w_c, proj_r, tm=512, tn=512):
    t = o_a.shape[0]
    d = D_MODEL
    gate = lambda part: pl.BlockSpec((tm, tn), lambda j, i: (i, (R_GATES + part * d) // tn + j))
    act = lambda width: pl.BlockSpec((tm, width), lambda j, i: (i, 0))
    wgt = lambda width: pl.BlockSpec((width, tn), lambda j, i: (0, j))
    return pl.pallas_call(
        _merge_kernel,
        out_shape=jax.ShapeDtypeStruct((t, d), BF16),
        grid=(d // tn, t // tm),
        in_specs=[act(W_A), act(W_B), act(C_CH), wgt(W_A), wgt(W_B), wgt(C_CH),
                  gate(0), gate(1), gate(2)],
        out_specs=pl.BlockSpec((tm, tn), lambda j, i: (i, j)),
        compiler_params=_cparams(),
        name="merge",
    )(o_a, o_b, o_c, w_a, w_b, w_c, proj_r, proj_r, proj_r)


def _mm_resid_kernel(y_ref, w_ref, x_ref, gate_ref, o_ref, acc_ref, *, nk):
    k = pl.program_id(2)

    @pl.when(k == 0)
    def _():
        acc_ref[...] = jnp.zeros_like(acc_ref)

    acc_ref[...] += _dot(y_ref[...], w_ref[...])

    @pl.when(k == nk - 1)
    def _():
        o_ref[...] = x_ref[...] + gate_ref[...] * acc_ref[...]


def _matmul_residual(y, w, x, mod4, gate_idx, *, n_ctx_rows, rows_per_latent, tm=1024, tn=1024, tk=512):
    t, kdim = y.shape
    n = w.shape[1]
    nk = kdim // tk
    row = lambda i: _mod_row(i * tm, n_ctx_rows, rows_per_latent)
    return pl.pallas_call(
        functools.partial(_mm_resid_kernel, nk=nk),
        out_shape=jax.ShapeDtypeStruct((t, n), F32),
        grid=(t // tm, n // tn, nk),
        in_specs=[pl.BlockSpec((tm, tk), lambda i, j, k: (i, k)),
                  pl.BlockSpec((tk, tn), lambda i, j, k: (k, j)),
                  pl.BlockSpec((tm, tn), lambda i, j, k: (i, j)),
                  pl.BlockSpec((None, None, 1, tn), lambda i, j, k: (row(i), gate_idx, 0, j))],
        out_specs=pl.BlockSpec((tm, tn), lambda i, j, k: (i, j)),
        scratch_shapes=[pltpu.VMEM((tm, tn), F32)],
        compiler_params=_cparams(),
        name="out_proj_residual",
    )(y, w, x, mod4)


def _router_kernel(x_ref, g_ref, scale_ref, shift_ref, wr_ref, br_ref, h_ref, ti_ref, tw_ref):
    x = x_ref[...]
    y = x * lax.rsqrt(jnp.mean(x * x, axis=-1, keepdims=True) + EPS)
    h = y * g_ref[...] * (1.0 + scale_ref[...]) + shift_ref[...]
    h_ref[...] = h
    logits = _dot_hl(h, wr_ref[...]) + br_ref[...]
    lane_i = lax.broadcasted_iota(jnp.int32, logits.shape, 1)
    lane = lane_i.astype(F32)
    neg = jnp.float32(-jnp.inf)
    cur = jnp.where(lane_i < N_EXPERTS, logits, neg)
    vals = []
    ti = jnp.zeros(logits.shape, F32)
    for kk in range(TOP_K):
        m = jnp.max(cur, axis=-1, keepdims=True)
        idx = jnp.min(jnp.where(cur == m, lane, float(LANES)), axis=-1, keepdims=True)
        ti = jnp.where(lane_i == kk, idx, ti)
        vals.append(m)
        cur = jnp.where(lane == idx, neg, cur)
    es = [jnp.exp(vk - vals[0]) for vk in vals]
    den = es[0] + es[1] + es[2] + es[3]
    tw = jnp.zeros(logits.shape, F32)
    for kk in range(TOP_K):
        tw = jnp.where(lane_i == kk, es[kk] / den, tw)
    ti_ref[...] = ti.astype(jnp.int32)
    tw_ref[...] = tw


def _router(x, g, mod4, shift_idx, scale_idx, w_router, b_router, *, n_ctx_rows, rows_per_latent, tm=256):
    t, d = x.shape
    row = lambda i: _mod_row(i * tm, n_ctx_rows, rows_per_latent)
    wr = jnp.pad(w_router, ((0, 0), (0, LANES - N_EXPERTS)))
    br = jnp.pad(b_router.reshape(1, -1), ((0, 0), (0, LANES - N_EXPERTS)))
    return pl.pallas_call(
        _router_kernel,
        out_shape=(jax.ShapeDtypeStruct((t, d), F32),
                   jax.ShapeDtypeStruct((t, LANES), jnp.int32),
                   jax.ShapeDtypeStruct((t, LANES), F32)),
        grid=(t // tm,),
        in_specs=[pl.BlockSpec((tm, d), lambda i: (i, 0)),
                  pl.BlockSpec((1, d), lambda i: (0, 0)),
                  pl.BlockSpec((None, None, 1, d), lambda i: (row(i), scale_idx, 0, 0)),
                  pl.BlockSpec((None, None, 1, d), lambda i: (row(i), shift_idx, 0, 0)),
                  pl.BlockSpec((d, LANES), lambda i: (0, 0)),
                  pl.BlockSpec((1, LANES), lambda i: (0, 0))],
        out_specs=(pl.BlockSpec((tm, d), lambda i: (i, 0)),
                   pl.BlockSpec((tm, LANES), lambda i: (i, 0)),
                   pl.BlockSpec((tm, LANES), lambda i: (i, 0))),
        compiler_params=_cparams(),
        name="router",
    )(x, g.reshape(1, d), mod4, mod4, wr, br)


def _row_copy(src_hbm, dst_vmem, sem, src_row, dst_row):
    return pltpu.make_async_copy(src_hbm.at[pl.ds(src_row, 1), :], dst_vmem.at[pl.ds(dst_row, 1), :], sem)


def _gather_kernel(src_ref, h_hbm, o_ref, buf, sem, *, tm):
    def start(r, _):
        _row_copy(h_hbm, buf, sem, src_ref[0, r], r).start()
        return 0

    lax.fori_loop(0, tm, start, 0)

    def wait(r, _):
        _row_copy(h_hbm, buf, sem, src_ref[0, r], r).wait()
        return 0

    lax.fori_loop(0, tm, wait, 0)
    o_ref[...] = buf[...].astype(o_ref.dtype)


def _moe_gather(h, src_token, n_rows, tm=256):
    d = h.shape[1]
    return pl.pallas_call(
        functools.partial(_gather_kernel, tm=tm),
        out_shape=jax.ShapeDtypeStruct((n_rows, d), BF16),
        grid=(n_rows // tm,),
        in_specs=[pl.BlockSpec((None, 1, tm), lambda i: (i, 0, 0), memory_space=pltpu.SMEM),
                  pl.BlockSpec(memory_space=pl.ANY)],
        out_specs=pl.BlockSpec((tm, d), lambda i: (i, 0)),
        scratch_shapes=[pltpu.VMEM((tm, d), F32), pltpu.SemaphoreType.DMA(())],
        compiler_params=_cparams(),
        name="moe_gather",
    )(src_token.reshape(n_rows // tm, 1, tm), h)


def _moe_up_kernel(te_ref, ta_ref, ts_ref, x_ref, wg_ref, wu_ref, bg_ref, bu_ref, rw_ref, o_ref, wg_s, wu_s):
    i = pl.program_id(1)
    changed = jnp.logical_or(i == 0, te_ref[i] != te_ref[jnp.maximum(i - 1, 0)])

    @pl.when(changed)
    def _():
        wg_s[...] = _bf(wg_ref[...])
        wu_s[...] = _bf(wu_ref[...])

    @pl.when(ta_ref[i] == 1)
    def _():
        x = x_ref[...]
        gate = jnp.dot(x, wg_s[...], preferred_element_type=F32) + bg_ref[...]
        up = jnp.dot(x, wu_s[...], preferred_element_type=F32) + bu_ref[...]
        gate = jnp.minimum(gate, SWIGLU_LIMIT)
        up = jnp.clip(up, -SWIGLU_LIMIT, SWIGLU_LIMIT)
        act = (up + 1.0) * gate * _sigmoid(SWIGLU_ALPHA * gate)
        o_ref[...] = (act * rw_ref[...]).astype(o_ref.dtype)

    @pl.when(ta_ref[i] == 0)
    def _():
        o_ref[...] = jnp.zeros_like(o_ref)


def _moe_up(x_sorted, w_gu, b_gu, row_w, tile_expert, tile_active, tile_src):
    p, d = x_sorted.shape
    tm, tf = MOE_TM, MOE_TF
    nf = D_FF // tf
    b3 = b_gu.reshape(N_EXPERTS, 1, 2 * D_FF)
    return pl.pallas_call(
        _moe_up_kernel,
        out_shape=jax.ShapeDtypeStruct((p, D_FF), BF16),
        grid_spec=pltpu.PrefetchScalarGridSpec(
            num_scalar_prefetch=3,
            grid=(nf, p // tm),
            in_specs=[pl.BlockSpec((tm, d), lambda j, i, te, ta, ts: (ts[i], 0)),
                      pl.BlockSpec((None, d, tf), lambda j, i, te, ta, ts: (te[i], 0, j)),
                      pl.BlockSpec((None, d, tf), lambda j, i, te, ta, ts: (te[i], 0, nf + j)),
                      pl.BlockSpec((None, 1, tf), lambda j, i, te, ta, ts: (te[i], 0, j)),
                      pl.BlockSpec((None, 1, tf), lambda j, i, te, ta, ts: (te[i], 0, nf + j)),
                      pl.BlockSpec((tm, 1), lambda j, i, te, ta, ts: (ts[i], 0))],
            out_specs=pl.BlockSpec((tm, tf), lambda j, i, te, ta, ts: (i, j)),
            scratch_shapes=[pltpu.VMEM((d, tf), BF16), pltpu.VMEM((d, tf), BF16)]),
        compiler_params=_cparams(),
        name="moe_up",
    )(tile_expert, tile_active, tile_src, x_sorted, w_gu, w_gu, b3, b3, row_w)


def _moe_down_kernel(te_ref, ta_ref, ts_ref, a_ref, w_ref, b_ref, rw_ref, o_ref, w_s):
    i = pl.program_id(1)
    changed = jnp.logical_or(i == 0, te_ref[i] != te_ref[jnp.maximum(i - 1, 0)])

    @pl.when(changed)
    def _():
        w_s[...] = _bf(w_ref[...])

    @pl.when(ta_ref[i] == 1)
    def _():
        o_ref[...] = (jnp.dot(a_ref[...], w_s[...], preferred_element_type=F32)
                      + rw_ref[...] * b_ref[...])

    @pl.when(ta_ref[i] == 0)
    def _():
        o_ref[...] = jnp.zeros_like(o_ref)


def _moe_down(act, w_down, b_down, row_w, tile_expert, tile_active, tile_src, tn=1024):
    p, f = act.shape
    d = w_down.shape[2]
    tm = MOE_TM
    b3 = b_down.reshape(N_EXPERTS, 1, d)
    return pl.pallas_call(
        _moe_down_kernel,
        out_shape=jax.ShapeDtypeStruct((p, d), F32),
        grid_spec=pltpu.PrefetchScalarGridSpec(
            num_scalar_prefetch=3,
            grid=(d // tn, p // tm),
            in_specs=[pl.BlockSpec((tm, f), lambda j, i, te, ta, ts: (ts[i], 0)),
                      pl.BlockSpec((None, f, tn), lambda j, i, te, ta, ts: (te[i], 0, j)),
                      pl.BlockSpec((None, 1, tn), lambda j, i, te, ta, ts: (te[i], 0, j)),
                      pl.BlockSpec((tm, 1), lambda j, i, te, ta, ts: (ts[i], 0))],
            out_specs=pl.BlockSpec((tm, tn), lambda j, i, te, ta, ts: (i, j)),
            scratch_shapes=[pltpu.VMEM((f, tn), BF16)]),
        compiler_params=_cparams(),
        name="moe_down",
    )(tile_expert, tile_active, tile_src, act, w_down, b3, row_w)


def _combine_kernel(dest_ref, y_hbm, x_ref, gate_ref, fg_ref, o_ref, buf, sem, *, tm, final_norm):
    def start(r, _):
        for kk in range(TOP_K):
            _row_copy(y_hbm, buf.at[kk], sem, dest_ref[0, r * TOP_K + kk], r).start()
        return 0

    lax.fori_loop(0, tm, start, 0)

    def wait(r, _):
        for kk in range(TOP_K):
            _row_copy(y_hbm, buf.at[kk], sem, dest_ref[0, r * TOP_K + kk], r).wait()
        return 0

    lax.fori_loop(0, tm, wait, 0)
    y = (buf[0] + buf[1]) + (buf[2] + buf[3])
    x = x_ref[...] + gate_ref[...] * y
    if final_norm:
        x = x * lax.rsqrt(jnp.mean(x * x, axis=-1, keepdims=True) + EPS) * fg_ref[...]
    o_ref[...] = x


def _moe_combine(y_sorted, dest, x, mod4, gate_idx, final_g, *, final_norm, n_ctx_rows, rows_per_latent):
    t, d = x.shape
    tm = COMBINE_TM
    row = lambda i: _mod_row(i * tm, n_ctx_rows, rows_per_latent)
    return pl.pallas_call(
        functools.partial(_combine_kernel, tm=tm, final_norm=final_norm),
        out_shape=jax.ShapeDtypeStruct((t, d), F32),
        grid=(t // tm,),
        in_specs=[pl.BlockSpec((None, 1, tm * TOP_K), lambda i: (i, 0, 0), memory_space=pltpu.SMEM),
                  pl.BlockSpec(memory_space=pl.ANY),
                  pl.BlockSpec((tm, d), lambda i: (i, 0)),
                  pl.BlockSpec((None, None, 1, d), lambda i: (row(i), gate_idx, 0, 0)),
                  pl.BlockSpec((1, d), lambda i: (0, 0))],
        out_specs=pl.BlockSpec((tm, d), lambda i: (i, 0)),
        scratch_shapes=[pltpu.VMEM((TOP_K, tm, d), F32), pltpu.SemaphoreType.DMA(())],
        compiler_params=_cparams(),
        name="moe_combine",
    )(dest.reshape(t // tm, 1, tm * TOP_K), y_sorted, x, mod4, final_g.reshape(1, d))


def _moe_plan(top_i, n_tiles):
    t = top_i.shape[0]
    tm = MOE_TM
    e_flat = top_i.reshape(-1)
    onehot = (e_flat[:, None] == jnp.arange(N_EXPERTS, dtype=jnp.int32)[None, :]).astype(jnp.int32)
    csum = jnp.cumsum(onehot, axis=0)
    counts = csum[-1]
    rank = jnp.sum(onehot * csum, axis=1) - 1
    tiles_per = (counts + tm - 1) // tm
    tile_end = jnp.cumsum(tiles_per)
    group_row0 = (tile_end - tiles_per) * tm
    dest = group_row0[e_flat] + rank
    n_used = tile_end[-1]
    tile_ids = jnp.arange(n_tiles, dtype=jnp.int32)
    tile_expert = jnp.searchsorted(tile_end, tile_ids, side="right").astype(jnp.int32)
    tile_active = (tile_ids < n_used).astype(jnp.int32)
    last_expert = jnp.searchsorted(tile_end, n_used - 1, side="right").astype(jnp.int32)
    tile_expert = jnp.where(tile_active == 1, tile_expert, last_expert)
    tile_src = jnp.minimum(tile_ids, n_used - 1)
    return dest.astype(jnp.int32), tile_expert, tile_active, tile_src


def _moe(x, norm_g, mod4, p, l, final_g, *, final_norm, n_ctx_rows, rows_per_latent):
    t = x.shape[0]
    blk = dict(n_ctx_rows=n_ctx_rows, rows_per_latent=rows_per_latent)
    h2, top_i, top_w = _router(x, norm_g, mod4, 3, 4, p["w_router"][l], p["b_router"][l], **blk)
    top_i, top_w = top_i[:, :TOP_K], top_w[:, :TOP_K]
    n_rows = t * TOP_K + N_EXPERTS * MOE_TM
    dest, tile_expert, tile_active, tile_src = _moe_plan(top_i, n_rows // MOE_TM)
    token_of_slot = jnp.arange(t * TOP_K, dtype=jnp.int32) // TOP_K
    src_token = jnp.zeros((n_rows,), jnp.int32).at[dest].set(token_of_slot)
    row_w = jnp.zeros((n_rows,), F32).at[dest].set(top_w.reshape(-1)).reshape(n_rows, 1)
    x_sorted = _moe_gather(h2, src_token, n_rows)
    act = _moe_up(x_sorted, p["w_gu"][l], p["b_gu"][l], row_w, tile_expert, tile_active, tile_src)
    y_sorted = _moe_down(act, p["w_down"][l], p["b_down"][l], row_w, tile_expert, tile_active, tile_src)
    return _moe_combine(y_sorted, dest, x, mod4, 5, final_g, final_norm=final_norm, **blk)


def _trunk(x, cvec, p, final_g, caches, *, n_ctx, ctx_len, n_lat, lat_len):
    n_ctx_rows = n_ctx * ctx_len
    blk = dict(n_ctx_rows=n_ctx_rows, rows_per_latent=lat_len)
    rope = _rope_tables(lat_len)
    dft = {}
    for n in (ctx_len, lat_len):
        fwd, inv = _dft_tables(n)
        fwd_hi = _bf(fwd)
        dft[n] = (fwd_hi, _bf(fwd - fwd_hi.astype(F32)), _bf(inv))
    silu_c = jax.nn.silu(cvec)
    outs = []
    for l in range(DEPTH):
        mod = _matmul(silu_c, p["w_mod"][l], n=N_MOD * D_MODEL, bias=p["b_mod"][l],
                      tm=16, tn=2048, tk=1024, name="modulation")
        mod4 = mod.reshape(16, N_MOD, 1, D_MODEL)
        h = _adaln(x, p["norm1_g"][l], mod4, 0, 1, **blk)
        w_in = p["w_in"][l]
        proj_a = _matmul(h, w_in, n=OFF_AB, tm=2048, tn=1024, tk=512, name="in_proj_a")
        w_ab = jnp.pad(w_in[:, OFF_AB:OFF_REST], ((0, 0), (0, LANES - 4 * H_A)))
        ab = _matmul(h, w_ab, n=LANES, tm=2048, tn=LANES, tk=2048, name="in_proj_ab")
        w_rest = _bf(w_in[:, OFF_REST:])
        proj_r = _matmul(h, w_rest, n=N_REST, tm=1024, tn=1536, tk=512, name="in_proj_rest")

        ck, cv, s0f, s0b = caches[l]
        gates = _gdn_gates(ab, p["a_log"][l], p["dt_bias"][l])
        zeros_state = jnp.zeros((n_ctx, H_A, DK_A, DK_A), F32)
        gdn = functools.partial(_gdn, proj_a, gates, p["conv_a"][l], p["onorm_a"][l])
        oa_c, sf_c, sb_c = gdn(zeros_state, zeros_state, n_seq=n_ctx, seq_len=ctx_len, row0=0)
        oa_l, _, _ = gdn(s0f, s0b, n_seq=n_lat, seq_len=lat_len, row0=n_ctx_rows)

        lam_init = 0.8 - 0.6 * math.exp(-0.3 * l)
        attn = functools.partial(_attention, proj_r, p["lam"][l], p["subln_b"][l], lam_init)
        ob_c = attn(n_seq=n_ctx, seq_len=ctx_len, row0=0)
        ob_l = attn(n_seq=n_lat, seq_len=lat_len, row0=n_ctx_rows, rope=rope, cache=(ck, cv))

        oc = []
        for n_seq, n, row0 in ((n_ctx, ctx_len, 0), (n_lat, lat_len, n_ctx_rows)):
            fwd_hi, fwd_lo, inv = dft[n]
            spec = _hyena_filters(n, p["filt_w1"][l], p["filt_b1"][l], p["filt_freq"][l],
                                  p["filt_w2"][l], p["filt_b2"][l], p["filt_w3"][l], fwd_hi, fwd_lo)
            oc.append(_hyena(proj_r, p["conv_c"][l], spec, p["filt_skip"][l], fwd_hi, inv,
                             n_seq=n_seq, seq_len=n, row0=row0))

        o_a = jnp.concatenate([oa_c, oa_l], axis=0)
        o_b = jnp.concatenate([ob_c, ob_l], axis=0)
        o_c = jnp.concatenate(oc, axis=0)
        merged = _merge(o_a, o_b, o_c, p["w_br_a"][l], p["w_br_b"][l], p["w_br_c"][l], proj_r)
        x = _matmul_residual(merged, p["w_out"][l], x, mod4, 2, **blk)
        x = _moe(x, p["norm2_g"][l], mod4, p, l, final_g, final_norm=(l == DEPTH - 1), **blk)

        kv = proj_r[:n_ctx_rows, R_KB:R_XC].reshape(n_ctx, ctx_len, 2, H_B, DV_B)
        outs.append((kv[:, :, 0], kv[:, :, 1], sf_c, sb_c))
    return x, outs


def kernel(x_prompt, x_sample, cache_k, cache_v, state_fwd, state_bwd, c, c_ctx, norm1_g, norm2_g, final_g, w_mod, b_mod, w_in, conv_a, a_log, dt_bias, onorm_a, lam, subln_b, conv_c, filt_w1, filt_b1, filt_freq, filt_w2, filt_b2, filt_w3, filt_skip, w_br_a, w_br_b, w_br_c, w_out, w_router, b_router, w_gu, b_gu, w_down, b_down):
    p = dict(norm1_g=norm1_g, norm2_g=norm2_g, w_mod=w_mod, b_mod=b_mod, w_in=w_in, conv_a=conv_a,
             a_log=a_log, dt_bias=dt_bias, onorm_a=onorm_a, lam=lam, subln_b=subln_b, conv_c=conv_c,
             filt_w1=filt_w1, filt_b1=filt_b1, filt_freq=filt_freq, filt_w2=filt_w2,
             filt_b2=filt_b2, filt_w3=filt_w3, filt_skip=filt_skip, w_br_a=w_br_a,
             w_br_b=w_br_b, w_br_c=w_br_c, w_out=w_out, w_router=w_router, b_router=b_router,
             w_gu=w_gu, b_gu=b_gu, w_down=w_down, b_down=b_down)
    n_ctx, ctx_len, d = x_prompt.shape
    n_lat, lat_len, _ = x_sample.shape
    past = cache_k.shape[2]
    x = jnp.concatenate([x_prompt.reshape(n_ctx * ctx_len, d), x_sample.reshape(n_lat * lat_len, d)], axis=0)
    cvec = jnp.concatenate([c_ctx[None, :], c, jnp.zeros((16 - 1 - n_lat, d), F32)], axis=0)
    caches = [(cache_k[:, l].reshape(n_lat, past, W_B), cache_v[:, l].reshape(n_lat, past, W_B),
               state_fwd[:, l], state_bwd[:, l]) for l in range(DEPTH)]
    y, outs = _trunk(x, cvec, p, final_g, caches, n_ctx=n_ctx, ctx_len=ctx_len, n_lat=n_lat, lat_len=lat_len)
    y_prompt = y[:n_ctx * ctx_len].reshape(n_ctx, ctx_len, d)
    y_sample = y[n_ctx * ctx_len:].reshape(n_lat, lat_len, d)
    stack = lambda idx: jnp.stack([o[idx] for o in outs], axis=1)
    return (y_prompt, y_sample, stack(0), stack(1), stack(2), stack(3))
```

```python
import functools
import math

import jax
import jax.numpy as jnp
from jax import lax
from jax.experimental import pallas as pl
from jax.experimental.pallas import tpu as pltpu

F32 = jnp.float32
BF16 = jnp.bfloat16

D_MODEL = 4096
DEPTH = 2
GRID_W = 64
H_A = 12
DK_A = 128
W_A = H_A * DK_A
SHORT_CONV = 3
H_B = 6
DK_B = 128
DV_B = 2 * DK_B
W_B = H_B * DV_B
ROPE_BASE = 10000.0
C_CH = 1024
HYENA_ORDER = 2
FILT_BANDS = 16
FILT_EMB = 1 + 2 * FILT_BANDS
FILT_HIDDEN = 64
HYENA_FAST_DECAY = 0.3
HYENA_SLOW_DECAY = 1.5
HYENA_TARGET = 1e-2
N_EXPERTS = 32
TOP_K = 4
D_FF = 1024
SWIGLU_LIMIT = 7.0
SWIGLU_ALPHA = 1.702
N_MOD = 6
EPS = 1e-6

OFF_AB = 4 * W_A
OFF_REST = OFF_AB + 4 * H_A
R_QB, R_KB, R_VB = 0, W_B, 2 * W_B
R_XC = 3 * W_B
R_GATES = R_XC + 3 * C_CH
N_REST = R_GATES + 3 * D_MODEL

LANES = 128
VMEM_LIMIT = 56 * 1024 * 1024
GDN_CHUNK = 256
GDN_HEADS = 2
ATT_QBLOCK = 256
MOE_TM = 512
MOE_TF = 256
COMBINE_TM = 128


def _cparams(**kw):
    return pltpu.CompilerParams(vmem_limit_bytes=VMEM_LIMIT, **kw)


def _bf(x):
    return x.astype(BF16)


def _dot(a, b):
    return jnp.dot(_bf(a), _bf(b), preferred_element_type=F32)


def _dot_nt(a, b):
    return lax.dot_general(_bf(a), _bf(b), (((1,), (1,)), ((), ())), preferred_element_type=F32)


def _dot_tn(a, b):
    return lax.dot_general(_bf(a), _bf(b), (((0,), (0,)), ((), ())), preferred_element_type=F32)


def _split2(x):
    hi = _bf(x)
    lo = _bf(x - hi.astype(F32))
    return hi, lo


def _split3(x):
    hi = _bf(x)
    r = x - hi.astype(F32)
    mid = _bf(r)
    lo = _bf(r - mid.astype(F32))
    return hi, mid, lo


def _dot_hl(a, b):
    ah, al = _split2(a)
    bh, bl = _split2(b)
    d = functools.partial(jnp.dot, preferred_element_type=F32)
    return d(ah, bh) + (d(ah, bl) + d(al, bh))


def _dot_exact_lhs(m01, x):
    m = _bf(m01)
    h, mi, lo = _split3(x)
    d = functools.partial(jnp.dot, preferred_element_type=F32)
    return d(m, h) + (d(m, mi) + d(m, lo))


def _sigmoid(x):
    return 1.0 / (1.0 + jnp.exp(-x))


def _silu(x):
    return x * _sigmoid(x)


def _softplus(x):
    return jnp.maximum(x, 0.0) + jnp.log(1.0 + jnp.exp(-jnp.abs(x)))


def _mod_row(row_start, n_ctx_rows, rows_per_latent):
    return jnp.where(row_start < n_ctx_rows, 0, 1 + (row_start - n_ctx_rows) // rows_per_latent)


def _mm_kernel(x_ref, w_ref, o_ref, acc_ref, *, nk):
    k = pl.program_id(2)

    @pl.when(k == 0)
    def _():
        acc_ref[...] = jnp.zeros_like(acc_ref)

    acc_ref[...] += _dot(x_ref[...], w_ref[...])

    @pl.when(k == nk - 1)
    def _():
        o_ref[...] = acc_ref[...].astype(o_ref.dtype)


def _mm_bias_kernel(x_ref, w_ref, b_ref, o_ref, acc_ref, *, nk):
    k = pl.program_id(2)

    @pl.when(k == 0)
    def _():
        acc_ref[...] = jnp.zeros_like(acc_ref)

    acc_ref[...] += _dot(x_ref[...], w_ref[...])

    @pl.when(k == nk - 1)
    def _():
        o_ref[...] = (acc_ref[...] + b_ref[...]).astype(o_ref.dtype)


def _layer_spec(block, index_map, layer):
    if layer is None:
        return pl.BlockSpec(block, index_map)
    return pl.BlockSpec((None,) + tuple(block), lambda *a: (layer,) + tuple(index_map(*a)))


def _matmul(x, w, *, n, col0=0, bias=None, tm, tn, tk, out_dtype=F32, name, layer=None):
    m, kdim = x.shape
    assert m % tm == 0 and n % tn == 0 and kdim % tk == 0 and col0 % tn == 0
    nk = kdim // tk
    cb = col0 // tn
    in_specs = [pl.BlockSpec((tm, tk), lambda i, j, k: (i, k)),
                _layer_spec((tk, tn), lambda i, j, k: (k, j + cb), layer)]
    args = [x, w]
    if bias is None:
        body = functools.partial(_mm_kernel, nk=nk)
    else:
        body = functools.partial(_mm_bias_kernel, nk=nk)
        in_specs.append(pl.BlockSpec((1, tn), lambda i, j, k: (0, j)))
        args.append(bias.reshape(1, n))
    return pl.pallas_call(
        body,
        out_shape=jax.ShapeDtypeStruct((m, n), out_dtype),
        grid=(m // tm, n // tn, nk),
        in_specs=in_specs,
        out_specs=pl.BlockSpec((tm, tn), lambda i, j, k: (i, j)),
        scratch_shapes=[pltpu.VMEM((tm, tn), F32)],
        compiler_params=_cparams(),
        name=name,
    )(*args)


def _adaln_kernel(x_ref, g_ref, scale_ref, shift_ref, o_ref):
    x = x_ref[...]
    y = x * lax.rsqrt(jnp.mean(x * x, axis=-1, keepdims=True) + EPS)
    o_ref[...] = (y * g_ref[...] * (1.0 + scale_ref[...]) + shift_ref[...]).astype(o_ref.dtype)


def _adaln(x, g, mod4, shift_idx, scale_idx, *, n_ctx_rows, rows_per_latent, tm=256):
    t, d = x.shape
    row = lambda i: _mod_row(i * tm, n_ctx_rows, rows_per_latent)
    return pl.pallas_call(
        _adaln_kernel,
        out_shape=jax.ShapeDtypeStruct((t, d), BF16),
        grid=(t // tm,),
        in_specs=[pl.BlockSpec((tm, d), lambda i: (i, 0)),
                  pl.BlockSpec((1, d), lambda i: (0, 0)),
                  pl.BlockSpec((None, None, 1, d), lambda i: (row(i), scale_idx, 0, 0)),
                  pl.BlockSpec((None, None, 1, d), lambda i: (row(i), shift_idx, 0, 0))],
        out_specs=pl.BlockSpec((tm, d), lambda i: (i, 0)),
        compiler_params=_cparams(),
        name="adaln",
    )(x, g.reshape(1, d), mod4, mod4)


def _gdn_gate_kernel(ab_ref, alog_ref, dtb_ref, o_ref):
    ab = ab_ref[...]
    tm = ab.shape[0]
    lane = lax.broadcasted_iota(jnp.int32, ab.shape, 1)
    rows = lax.broadcasted_iota(jnp.int32, (tm, tm), 0)
    cols = lax.broadcasted_iota(jnp.int32, (tm, tm), 1)
    g = -jnp.exp(alog_ref[...]) * _softplus(ab + dtb_ref[...])
    prefix = _dot_exact_lhs((cols <= rows).astype(F32), g)
    suffix = _dot_exact_lhs((cols >= rows).astype(F32), g)
    gcum = jnp.where(lane < H_A, prefix, suffix)
    o_ref[...] = jnp.where(lane < 2 * H_A, gcum, _sigmoid(ab))


def _gdn_gates(ab, a_log, dt_bias):
    t = ab.shape[0]
    tm = GDN_CHUNK
    pad = lambda v: jnp.pad(v.reshape(1, 2 * H_A), ((0, 0), (0, LANES - 2 * H_A)))
    return pl.pallas_call(
        _gdn_gate_kernel,
        out_shape=jax.ShapeDtypeStruct((t, LANES), F32),
        grid=(t // tm,),
        in_specs=[pl.BlockSpec((tm, LANES), lambda i: (i, 0)),
                  pl.BlockSpec((1, LANES), lambda i: (0, 0)),
                  pl.BlockSpec((1, LANES), lambda i: (0, 0))],
        out_specs=pl.BlockSpec((tm, LANES), lambda i: (i, 0)),
        compiler_params=_cparams(),
        name="gdn_gates",
    )(ab, pad(a_log), pad(dt_bias))


def _unit_tri_inverse(mats, rows, cols, dot):
    n = mats[0].shape[0]
    eye = (rows == cols).astype(F32)
    same = lambda s: (rows // s) == (cols // s)
    dps = [jnp.where(same(16), a, 0.0) for a in mats]
    ts = [eye - d for d in dps]
    for _ in range(3):
        dbs = [_bf(d) for d in dps]
        dps = [dot(d, d) for d in dbs]
        ts = [t + dot(t, d) for t, d in zip(ts, dps)]
    s = 16
    while s < n:
        mask = same(2 * s) & jnp.logical_not(same(s))
        tbs = [_bf(t) for t in ts]
        lts = [dot(jnp.where(mask, a, 0.0), tb) for a, tb in zip(mats, tbs)]
        ts = [t - dot(tb, lt) for t, tb, lt in zip(ts, tbs, lts)]
        s *= 2
    return ts


def _gdn_kernel(q_ref, k_ref, v_ref, z_ref, gate_ref, cw_ref, og_ref, s0f_ref, s0b_ref,
                o_ref, sf_ref, sb_ref, qs, ks, vs, of_s, ob_s, *, seq_len, chunk, heads):
    h0 = pl.program_id(1) * heads
    n_chunks = seq_len // chunk
    width = heads * DK_A
    head = lambda j: slice(j * DK_A, (j + 1) * DK_A)

    pos = lax.broadcasted_iota(jnp.int32, (seq_len, width), 0)

    def conv_silu(x_ref, w):
        x = x_ref[...]
        prev = jnp.where(pos == 0, 0.0, pltpu.roll(x, 1, 0))
        nxt = jnp.where(pos == seq_len - 1, 0.0, pltpu.roll(x, seq_len - 1, 0))
        return _silu(prev * w[0:1, :] + x * w[1:2, :] + nxt * w[2:3, :])

    def l2n(x):
        return x * lax.rsqrt(jnp.sum(x * x, axis=-1, keepdims=True) + 1e-6)

    qc = conv_silu(q_ref, cw_ref[0])
    kc = conv_silu(k_ref, cw_ref[1])
    for j in range(heads):
        qs[:, head(j)] = l2n(qc[:, head(j)]) * (DK_A ** -0.5)
        ks[:, head(j)] = l2n(kc[:, head(j)])
    vs[...] = conv_silu(v_ref, cw_ref[2])

    rows = lax.broadcasted_iota(jnp.int32, (chunk, chunk), 0)
    cols = lax.broadcasted_iota(jnp.int32, (chunk, chunk), 1)
    lane = lax.broadcasted_iota(jnp.int32, (chunk, LANES), 1)

    chains = [(j, d) for j in range(heads) for d in (0, 1)]
    masks = {0: (cols <= rows, cols < rows),
             1: (cols >= rows, cols > rows)}
    out_refs = {0: of_s, 1: ob_s}

    def body(i, states):
        r0s = {0: pl.multiple_of(i * chunk, chunk), 1: pl.multiple_of((n_chunks - 1 - i) * chunk, chunk)}
        gates = {d: gate_ref[pl.ds(r0s[d], chunk), :] for d in (0, 1)}
        pick = lambda d, idx: jnp.sum(jnp.where(lane == idx, gates[d], 0.0), axis=-1, keepdims=True)
        qkv = [tuple(s[pl.ds(r0s[d], chunk), head(j)] for s in (qs, ks, vs)) for j, d in chains]
        gcum = [jnp.broadcast_to(pick(d, d * H_A + h0 + j), (chunk, LANES)) for j, d in chains]
        beta = [pick(d, (2 + d) * H_A + h0 + j) for j, d in chains]
        last = {0: chunk - 1, 1: 0}
        g_last = [gcm[last[d]:last[d] + 1, :] for (j, d), gcm in zip(chains, gcum)]
        decay = []
        for (j, d), gcm in zip(chains, gcum):
            gc = jnp.concatenate([gcm] * (chunk // LANES), axis=1)
            incl = masks[d][0]
            decay.append(jnp.where(incl, jnp.exp(jnp.where(incl, gc - gc.T, 0.0)), 0.0))
        kk = [_dot_nt(k, k) for q, k, v in qkv]
        a = [jnp.where(masks[d][1], b * x * dc, 0.0) for (j, d), b, x, dc in zip(chains, beta, kk, decay)]
        t = _unit_tri_inverse(a, rows, cols, _dot)
        rhs = [jnp.concatenate([v * b, k * b * jnp.exp(gcm)], axis=-1)
               for (q, k, v), b, gcm in zip(qkv, beta, gcum)]
        uw = [_dot(ti, r) for ti, r in zip(t, rhs)]
        qk = [_dot_nt(q, k) * dc for (q, k, v), dc in zip(qkv, decay)]
        ws = [_dot(jnp.concatenate([x[:, DK_A:], q * jnp.exp(gcm)], axis=0), s)
              for x, (q, k, v), gcm, s in zip(uw, qkv, gcum, states)]
        v_new = [x[:, :DK_A] - y[:chunk] for x, y in zip(uw, ws)]
        o = [y[chunk:] + _dot(m, vn) for y, m, vn in zip(ws, qk, v_new)]
        for (j, d), oi in zip(chains, o):
            out_refs[d][pl.ds(r0s[d], chunk), head(j)] = oi
        return tuple(s * jnp.exp(gl[:, 0:1]) + _dot_tn(k * jnp.exp(gl - gcm), vn)
                     for s, gl, (q, k, v), gcm, vn in zip(states, g_last, qkv, gcum, v_new))

    init = tuple((s0f_ref, s0b_ref)[d][j] for j, d in chains)
    final = lax.fori_loop(0, n_chunks, body, init)
    for (j, d), s in zip(chains, final):
        (sf_ref, sb_ref)[d][j] = s
    for j in range(heads):
        o = of_s[:, head(j)] + ob_s[:, head(j)]
        o = o * lax.rsqrt(jnp.mean(o * o, axis=-1, keepdims=True) + EPS) * og_ref[...]
        o_ref[:, head(j)] = (o * _silu(z_ref[:, head(j)])).astype(o_ref.dtype)


def _gdn(proj_a, gates, conv_w, onorm_g, s0f, s0b, *, n_seq, seq_len, row0, heads=GDN_HEADS):
    assert row0 % seq_len == 0 and seq_len % GDN_CHUNK == 0 and H_A % heads == 0
    rb = row0 // seq_len
    ng = H_A // heads
    width = heads * DK_A
    cw = conv_w.reshape(SHORT_CONV, 3, W_A).transpose(1, 0, 2)
    tok = lambda part: pl.BlockSpec((seq_len, width), lambda b, h: (b + rb, part * ng + h))
    st = pl.BlockSpec((None, heads, DK_A, DK_A), lambda b, h: (b, h, 0, 0))
    body = functools.partial(_gdn_kernel, seq_len=seq_len, chunk=GDN_CHUNK, heads=heads)
    return pl.pallas_call(
        body,
        out_shape=(jax.ShapeDtypeStruct((n_seq * seq_len, W_A), BF16),
                   jax.ShapeDtypeStruct((n_seq, H_A, DK_A, DK_A), F32),
                   jax.ShapeDtypeStruct((n_seq, H_A, DK_A, DK_A), F32)),
        grid=(n_seq, ng),
        in_specs=[tok(0), tok(1), tok(2), tok(3),
                  pl.BlockSpec((seq_len, LANES), lambda b, h: (b + rb, 0)),
                  pl.BlockSpec((3, SHORT_CONV, width), lambda b, h: (0, 0, h)),
                  pl.BlockSpec((1, DK_A), lambda b, h: (0, 0)),
                  st, st],
        out_specs=(pl.BlockSpec((seq_len, width), lambda b, h: (b, h)), st, st),
        scratch_shapes=[pltpu.VMEM((seq_len, width), F32) for _ in range(5)],
        compiler_params=_cparams(),
        name=f"gdn_L{seq_len}",
    )(proj_a, proj_a, proj_a, proj_a, gates, cw, onorm_g.reshape(1, DK_A), s0f, s0b)


def _rope(x, cos, sin_signed, lane):
    rot = jnp.where((lane % 64) < 32, pltpu.roll(x, LANES - 32, 1), pltpu.roll(x, 32, 1))
    return x * cos + rot * sin_signed


def _attn_kernel(*refs, seq_len, qblock, use_rope, n_cache, lam_init):
    it = iter(refs)
    q_ref, k_ref, v_ref, lam_ref, g_ref = next(it), next(it), next(it), next(it), next(it)
    cos_ref = sin_ref = ck_ref = cv_ref = None
    if use_rope:
        cos_ref, sin_ref = next(it), next(it)
    if n_cache:
        ck_ref, cv_ref = next(it), next(it)
    o_ref = next(it)
    ks = next(it)

    lam = lam_ref[...]
    lam_full = (jnp.exp(jnp.sum(lam[0:1] * lam[1:2], axis=-1, keepdims=True))
                - jnp.exp(jnp.sum(lam[2:3] * lam[3:4], axis=-1, keepdims=True)) + lam_init)
    scale = DK_B ** -0.5
    lane = lax.broadcasted_iota(jnp.int32, (seq_len, DK_B), 1) if use_rope else None
    lane_q = lax.broadcasted_iota(jnp.int32, (qblock, DK_B), 1) if use_rope else None
    for r in range(2):
        kr = k_ref[:, r * DK_B:(r + 1) * DK_B]
        if use_rope:
            kr = _rope(kr, cos_ref[...], sin_ref[...], lane)
        ks[r] = _bf(kr)
    v = _bf(v_ref[...])
    for qb in range(seq_len // qblock):
        sl = slice(qb * qblock, (qb + 1) * qblock)
        probs = []
        for r in range(2):
            qr = q_ref[sl, r * DK_B:(r + 1) * DK_B]
            if use_rope:
                qr = _rope(qr, cos_ref[sl, :], sin_ref[sl, :], lane_q)
            s = _dot_nt(qr, ks[r]) * scale
            m = jnp.max(s, axis=-1, keepdims=True)
            if n_cache:
                sc = _dot_nt(qr, ck_ref[:, r * DK_B:(r + 1) * DK_B]) * scale
                m = jnp.maximum(m, jnp.max(sc, axis=-1, keepdims=True))
                ec = jnp.exp(sc - m)
            e = jnp.exp(s - m)
            den = jnp.sum(e, axis=-1, keepdims=True)
            if n_cache:
                den = den + jnp.sum(ec, axis=-1, keepdims=True)
                probs.append((e / den, ec / den))
            else:
                probs.append((e / den, None))
        o = _dot(probs[0][0] - lam_full * probs[1][0], v)
        if n_cache:
            o = o + _dot(probs[0][1] - lam_full * probs[1][1], cv_ref[...])
        o = o * lax.rsqrt(jnp.mean(o * o, axis=-1, keepdims=True) + 1e-5) * g_ref[...]
        o_ref[sl, :] = (o * (1.0 - lam_init)).astype(o_ref.dtype)


def _attention(proj_r, lam, subln_g, lam_init, *, n_seq, seq_len, row0, rope=None, cache=None):
    assert row0 % seq_len == 0
    rb = row0 // seq_len
    nh = H_B
    blk = lambda part: pl.BlockSpec((seq_len, DV_B), lambda b, h: (b + rb, part * nh + h))
    in_specs = [blk(0), blk(1), blk(2),
                pl.BlockSpec((4, DK_B), lambda b, h: (0, 0)),
                pl.BlockSpec((1, DV_B), lambda b, h: (0, 0))]
    args = [proj_r, proj_r, proj_r, lam, subln_g.reshape(1, DV_B)]
    if rope is not None:
        in_specs += [pl.BlockSpec((seq_len, DK_B), lambda b, h: (0, 0))] * 2
        args += list(rope)
    n_cache = 0
    if cache is not None:
        ck, cv = cache
        n_cache = ck.shape[1]
        in_specs += [pl.BlockSpec((None, n_cache, DV_B), lambda b, h: (b, 0, h))] * 2
        args += [ck, cv]
    body = functools.partial(_attn_kernel, seq_len=seq_len, qblock=min(ATT_QBLOCK, seq_len),
                             use_rope=rope is not None, n_cache=n_cache, lam_init=lam_init)
    return pl.pallas_call(
        body,
        out_shape=jax.ShapeDtypeStruct((n_seq * seq_len, W_B), BF16),
        grid=(n_seq, nh),
        in_specs=in_specs,
        out_specs=pl.BlockSpec((seq_len, DV_B), lambda b, h: (b, h)),
        scratch_shapes=[pltpu.VMEM((2, seq_len, DK_B), BF16)],
        compiler_params=_cparams(),
        name=f"diff_attn_L{seq_len}",
    )(*args)


def _rope_tables(n_tok):
    rows = n_tok // GRID_W
    row = jnp.repeat(jnp.arange(rows), GRID_W)
    col = jnp.tile(jnp.arange(GRID_W), rows)
    half = DK_B // 2
    inv = ROPE_BASE ** (-jnp.arange(0, half, 2, dtype=F32) / half)
    ang = jnp.stack([row, col], axis=-1).astype(F32)[..., None] * inv
    cos, sin = jnp.cos(ang), jnp.sin(ang)
    cos_t = jnp.concatenate([cos, cos], axis=-1).reshape(n_tok, DK_B)
    sin_t = jnp.concatenate([-sin, sin], axis=-1).reshape(n_tok, DK_B)
    return cos_t, sin_t


def _dft_tables(n):
    f = jnp.arange(n, dtype=jnp.int32)[:, None]
    t = jnp.arange(n, dtype=jnp.int32)[None, :]
    ang = ((f * t) % (2 * n)).astype(F32) * (math.pi / n)
    cos, sin = jnp.cos(ang), jnp.sin(ang)
    nyq = jnp.where(t % 2 == 0, 1.0, -1.0).astype(F32)
    fwd_im = jnp.where(f == 0, nyq, -sin)
    fwd = jnp.concatenate([cos, fwd_im], axis=0)
    wgt = jnp.where(f == 0, 1.0, 2.0).astype(F32) / (2 * n)
    inv_re = (wgt * cos).T
    inv_im = jnp.where(f == 0, nyq / (2 * n), -wgt * sin).T
    inv = jnp.concatenate([inv_re, inv_im], axis=1)
    return fwd, inv


def _spec_mul(u, s, n, row):
    ur, ui, sr, si = u[:n], u[n:], s[:n], s[n:]
    first = row == 0
    yr = ur * sr - jnp.where(first, 0.0, ui * si)
    yi = jnp.where(first, ui * si, ur * si + ui * sr)
    return yr, yi


def _hyena_filter_kernel(z_ref, w1_ref, b1_ref, fr_ref, w2_ref, b2_ref, w3_ref, win_ref,
                         fh_ref, fl_ref, o_ref, *, n):
    fr = fr_ref[...]
    h = jnp.sin(fr * (_dot_hl(z_ref[...], w1_ref[...]) + b1_ref[...]))
    h = jnp.sin(fr * (_dot_hl(h, w2_ref[...]) + b2_ref[...]))
    win = win_ref[...]
    row = lax.broadcasted_iota(jnp.int32, win.shape, 0)
    fwd_hi, fwd_lo = fh_ref[...], fl_ref[...]

    def dft(x):
        xh, xl = _split2(x)
        d = functools.partial(jnp.dot, preferred_element_type=F32)
        return d(fwd_hi, xh) + (d(fwd_hi, xl) + d(fwd_lo, xh))

    for o in range(HYENA_ORDER):
        hf = _dot_hl(h, w3_ref[2 * o]) * win
        hb = jnp.where(row == 0, 0.0, _dot_hl(h, w3_ref[2 * o + 1]) * win)
        p, q = dft(hf), dft(hb)
        o_ref[o, :n, :] = p[:n] + q[:n]
        o_ref[o, n:, :] = jnp.where(row == 0, p[n:] + q[n:], p[n:] - q[n:])


def _hyena_filters(n, w1, b1, freq, w2, b2, w3, fwd_hi, fwd_lo, tc=256):
    t = jnp.linspace(0.0, 1.0, n, dtype=F32)[:, None]
    wpos = 2.0 * math.pi * jnp.arange(n, dtype=F32)[:, None] / n
    f = jnp.linspace(1e-4, FILT_BANDS - 1, FILT_BANDS, dtype=F32)
    z = jnp.concatenate([t, jnp.cos(wpos * f), -jnp.sin(wpos * f)], axis=-1)
    z = jnp.pad(z, ((0, 0), (0, LANES - FILT_EMB)))
    w1p = jnp.pad(w1, ((0, LANES - FILT_EMB), (0, 0)))
    max_decay = math.log(HYENA_TARGET) / HYENA_FAST_DECAY
    min_decay = math.log(HYENA_TARGET) / HYENA_SLOW_DECAY
    deltas = jnp.linspace(min_decay, max_decay, C_CH, dtype=F32)
    window = jnp.exp(-t * jnp.abs(deltas))
    w3r = w3.reshape(FILT_HIDDEN, 2 * HYENA_ORDER, C_CH).transpose(1, 0, 2)
    full = lambda shape: pl.BlockSpec(shape, lambda j: (0,) * len(shape))
    return pl.pallas_call(
        functools.partial(_hyena_filter_kernel, n=n),
        out_shape=jax.ShapeDtypeStruct((HYENA_ORDER, 2 * n, C_CH), F32),
        grid=(C_CH // tc,),
        in_specs=[full((n, LANES)), full((LANES, FILT_HIDDEN)), full((1, FILT_HIDDEN)),
                  full((1, FILT_HIDDEN)), full((FILT_HIDDEN, FILT_HIDDEN)), full((1, FILT_HIDDEN)),
                  pl.BlockSpec((2 * HYENA_ORDER, FILT_HIDDEN, tc), lambda j: (0, 0, j)),
                  pl.BlockSpec((n, tc), lambda j: (0, j)),
                  full((2 * n, n)), full((2 * n, n))],
        out_specs=pl.BlockSpec((HYENA_ORDER, 2 * n, tc), lambda j: (0, 0, j)),
        compiler_params=_cparams(),
        name=f"hyena_filter_L{n}",
    )(z, w1p, b1.reshape(1, -1), freq.reshape(1, -1), w2, b2.reshape(1, -1), w3r, window,
      fwd_hi, fwd_lo)


def _hyena_kernel(x1_ref, x2_ref, v_ref, cw_ref, spec_ref, skip_ref, fwd_ref, inv_ref, o_ref, *, n):
    shape = v_ref.shape
    row = lax.broadcasted_iota(jnp.int32, shape, 0)

    def conv3(x_ref, p):
        x = x_ref[...]
        prev = jnp.where(row == 0, 0.0, pltpu.roll(x, 1, 0))
        nxt = jnp.where(row == n - 1, 0.0, pltpu.roll(x, n - 1, 0))
        return prev * cw_ref[p, 0:1, :] + x * cw_ref[p, 1:2, :] + nxt * cw_ref[p, 2:3, :]

    z = conv3(v_ref, 2)
    fwd, inv = fwd_ref[...], inv_ref[...]
    for o, gate_ref in enumerate((x1_ref, x2_ref)):
        u = jnp.dot(fwd, _bf(z), preferred_element_type=F32)
        yr, yi = _spec_mul(u, spec_ref[o], n, row)
        y = jnp.dot(inv, _bf(jnp.concatenate([yr, yi], axis=0)), preferred_element_type=F32)
        z = conv3(gate_ref, o) * (y + z * skip_ref[o:o + 1, :])
    o_ref[...] = z.astype(o_ref.dtype)


def _hyena(proj_r, conv_w, spec, skip, fwd, inv, *, n_seq, seq_len, row0, tc=256):
    assert row0 % seq_len == 0
    rb = row0 // seq_len
    xb = lambda part: pl.BlockSpec((seq_len, tc), lambda b, j: (b + rb, (R_XC + part * C_CH) // tc + j))
    cw = conv_w.reshape(SHORT_CONV, 3, C_CH).transpose(1, 0, 2)
    return pl.pallas_call(
        functools.partial(_hyena_kernel, n=seq_len),
        out_shape=jax.ShapeDtypeStruct((n_seq * seq_len, C_CH), BF16),
        grid=(n_seq, C_CH // tc),
        in_specs=[xb(0), xb(1), xb(2),
                  pl.BlockSpec((3, SHORT_CONV, tc), lambda b, j: (0, 0, j)),
                  pl.BlockSpec((HYENA_ORDER, 2 * seq_len, tc), lambda b, j: (0, 0, j)),
                  pl.BlockSpec((HYENA_ORDER, tc), lambda b, j: (0, j)),
                  pl.BlockSpec((2 * seq_len, seq_len), lambda b, j: (0, 0)),
                  pl.BlockSpec((seq_len, 2 * seq_len), lambda b, j: (0, 0))],
        out_specs=pl.BlockSpec((seq_len, tc), lambda b, j: (b, j)),
        compiler_params=_cparams(),
        name=f"hyena_L{seq_len}",
    )(proj_r, proj_r, proj_r, cw, spec, skip, fwd, inv)


def _merge_kernel(oa_ref, ob_ref, oc_ref, wa_ref, wb_ref, wc_ref, ga_ref, gb_ref, gc_ref, o_ref):
    acc = _sigmoid(ga_ref[...]) * _dot(oa_ref[...], wa_ref[...])
    acc += _sigmoid(gb_ref[...]) * _dot(ob_ref[...], wb_ref[...])
    acc += _sigmoid(gc_ref[...]) * _dot(oc_ref[...], wc_ref[...])
    o_ref[...] = acc.astype(o_ref.dtype)


def _merge(o_a, o_b, o_c, w_a, w_b, w_c, proj_r, layer, tm=512, tn=512):
    t = o_a.shape[0]
    d = D_MODEL
    gate = lambda part: pl.BlockSpec((tm, tn), lambda j, i: (i, (R_GATES + part * d) // tn + j))
    act = lambda width: pl.BlockSpec((tm, width), lambda j, i: (i, 0))
    wgt = lambda width: _layer_spec((width, tn), lambda j, i: (0, j), layer)
    return pl.pallas_call(
        _merge_kernel,
        out_shape=jax.ShapeDtypeStruct((t, d), BF16),
        grid=(d // tn, t // tm),
        in_specs=[act(W_A), act(W_B), act(C_CH), wgt(W_A), wgt(W_B), wgt(C_CH),
                  gate(0), gate(1), gate(2)],
        out_specs=pl.BlockSpec((tm, tn), lambda j, i: (i, j)),
        compiler_params=_cparams(),
        name="merge",
    )(o_a, o_b, o_c, w_a, w_b, w_c, proj_r, proj_r, proj_r)


def _mm_resid_kernel(y_ref, w_ref, x_ref, gate_ref, o_ref, acc_ref, *, nk):
    k = pl.program_id(2)

    @pl.when(k == 0)
    def _():
        acc_ref[...] = jnp.zeros_like(acc_ref)

    acc_ref[...] += _dot(y_ref[...], w_ref[...])

    @pl.when(k == nk - 1)
    def _():
        o_ref[...] = x_ref[...] + gate_ref[...] * acc_ref[...]


def _matmul_residual(y, w, x, mod4, gate_idx, layer, *, n_ctx_rows, rows_per_latent, tm=1024, tn=1024, tk=512):
    t, kdim = y.shape
    n = w.shape[-1]
    nk = kdim // tk
    row = lambda i: _mod_row(i * tm, n_ctx_rows, rows_per_latent)
    return pl.pallas_call(
        functools.partial(_mm_resid_kernel, nk=nk),
        out_shape=jax.ShapeDtypeStruct((t, n), F32),
        grid=(t // tm, n // tn, nk),
        in_specs=[pl.BlockSpec((tm, tk), lambda i, j, k: (i, k)),
                  _layer_spec((tk, tn), lambda i, j, k: (k, j), layer),
                  pl.BlockSpec((tm, tn), lambda i, j, k: (i, j)),
                  pl.BlockSpec((None, None, 1, tn), lambda i, j, k: (row(i), gate_idx, 0, j))],
        out_specs=pl.BlockSpec((tm, tn), lambda i, j, k: (i, j)),
        scratch_shapes=[pltpu.VMEM((tm, tn), F32)],
        compiler_params=_cparams(),
        name="out_proj_residual",
    )(y, w, x, mod4)


def _router_kernel(x_ref, g_ref, scale_ref, shift_ref, wr_ref, br_ref, h_ref, ti_ref, tw_ref):
    x = x_ref[...]
    y = x * lax.rsqrt(jnp.mean(x * x, axis=-1, keepdims=True) + EPS)
    h = y * g_ref[...] * (1.0 + scale_ref[...]) + shift_ref[...]
    h_ref[...] = h
    logits = _dot_hl(h, wr_ref[...]) + br_ref[...]
    lane_i = lax.broadcasted_iota(jnp.int32, logits.shape, 1)
    lane = lane_i.astype(F32)
    neg = jnp.float32(-jnp.inf)
    cur = jnp.where(lane_i < N_EXPERTS, logits, neg)
    vals = []
    ti = jnp.zeros(logits.shape, F32)
    for kk in range(TOP_K):
        m = jnp.max(cur, axis=-1, keepdims=True)
        idx = jnp.min(jnp.where(cur == m, lane, float(LANES)), axis=-1, keepdims=True)
        ti = jnp.where(lane_i == kk, idx, ti)
        vals.append(m)
        cur = jnp.where(lane == idx, neg, cur)
    es = [jnp.exp(vk - vals[0]) for vk in vals]
    den = es[0] + es[1] + es[2] + es[3]
    tw = jnp.zeros(logits.shape, F32)
    for kk in range(TOP_K):
        tw = jnp.where(lane_i == kk, es[kk] / den, tw)
    ti_ref[...] = ti.astype(jnp.int32)
    tw_ref[...] = tw


def _router(x, g, mod4, shift_idx, scale_idx, w_router, b_router, *, n_ctx_rows, rows_per_latent, tm=256):
    t, d = x.shape
    row = lambda i: _mod_row(i * tm, n_ctx_rows, rows_per_latent)
    wr = jnp.pad(w_router, ((0, 0), (0, LANES - N_EXPERTS)))
    br = jnp.pad(b_router.reshape(1, -1), ((0, 0), (0, LANES - N_EXPERTS)))
    return pl.pallas_call(
        _router_kernel,
        out_shape=(jax.ShapeDtypeStruct((t, d), F32),
                   jax.ShapeDtypeStruct((t, LANES), jnp.int32),
                   jax.ShapeDtypeStruct((t, LANES), F32)),
        grid=(t // tm,),
        in_specs=[pl.BlockSpec((tm, d), lambda i: (i, 0)),
                  pl.BlockSpec((1, d), lambda i: (0, 0)),
                  pl.BlockSpec((None, None, 1, d), lambda i: (row(i), scale_idx, 0, 0)),
                  pl.BlockSpec((None, None, 1, d), lambda i: (row(i), shift_idx, 0, 0)),
                  pl.BlockSpec((d, LANES), lambda i: (0, 0)),
                  pl.BlockSpec((1, LANES), lambda i: (0, 0))],
        out_specs=(pl.BlockSpec((tm, d), lambda i: (i, 0)),
                   pl.BlockSpec((tm, LANES), lambda i: (i, 0)),
                   pl.BlockSpec((tm, LANES), lambda i: (i, 0))),
        compiler_params=_cparams(),
        name="router",
    )(x, g.reshape(1, d), mod4, mod4, wr, br)


def _row_copy(src_hbm, dst_vmem, sem, src_row, dst_row):
    return pltpu.make_async_copy(src_hbm.at[pl.ds(src_row, 1), :], dst_vmem.at[pl.ds(dst_row, 1), :], sem)


def _gather_kernel(ta_ref, src_ref, h_hbm, o_ref, buf, sem, *, tm):
    active = ta_ref[pl.program_id(0)] == 1

    @pl.when(active)
    def _():
        def start(r, _):
            _row_copy(h_hbm, buf, sem, src_ref[0, r], r).start()
            return 0

        lax.fori_loop(0, tm, start, 0)

        def wait(r, _):
            _row_copy(h_hbm, buf, sem, src_ref[0, r], r).wait()
            return 0

        lax.fori_loop(0, tm, wait, 0)
        o_ref[...] = buf[...].astype(o_ref.dtype)

    @pl.when(jnp.logical_not(active))
    def _():
        o_ref[...] = jnp.zeros_like(o_ref)


def _moe_gather(h, src_token, tile_active, n_rows):
    d = h.shape[1]
    tm = MOE_TM
    return pl.pallas_call(
        functools.partial(_gather_kernel, tm=tm),
        out_shape=jax.ShapeDtypeStruct((n_rows, d), BF16),
        grid_spec=pltpu.PrefetchScalarGridSpec(
            num_scalar_prefetch=1,
            grid=(n_rows // tm,),
            in_specs=[pl.BlockSpec((None, 1, tm), lambda i, ta: (i, 0, 0), memory_space=pltpu.SMEM),
                      pl.BlockSpec(memory_space=pl.ANY)],
            out_specs=pl.BlockSpec((tm, d), lambda i, ta: (i, 0)),
            scratch_shapes=[pltpu.VMEM((tm, d), F32), pltpu.SemaphoreType.DMA(())]),
        compiler_params=_cparams(),
        name="moe_gather",
    )(tile_active, src_token.reshape(n_rows // tm, 1, tm), h)


def _moe_up_kernel(te_ref, ta_ref, ts_ref, x_ref, wg_ref, wu_ref, bg_ref, bu_ref, rw_ref, o_ref, wg_s, wu_s):
    i = pl.program_id(1)
    changed = jnp.logical_or(i == 0, te_ref[i] != te_ref[jnp.maximum(i - 1, 0)])

    @pl.when(changed)
    def _():
        wg_s[...] = _bf(wg_ref[...])
        wu_s[...] = _bf(wu_ref[...])

    @pl.when(ta_ref[i] == 1)
    def _():
        x = x_ref[...]
        gate = jnp.dot(x, wg_s[...], preferred_element_type=F32) + bg_ref[...]
        up = jnp.dot(x, wu_s[...], preferred_element_type=F32) + bu_ref[...]
        gate = jnp.minimum(gate, SWIGLU_LIMIT)
        up = jnp.clip(up, -SWIGLU_LIMIT, SWIGLU_LIMIT)
        act = (up + 1.0) * gate * _sigmoid(SWIGLU_ALPHA * gate)
        o_ref[...] = (act * rw_ref[...]).astype(o_ref.dtype)

    @pl.when(ta_ref[i] == 0)
    def _():
        o_ref[...] = jnp.zeros_like(o_ref)


def _moe_up(x_sorted, w_gu, b_gu, row_w, tile_expert, tile_active, tile_src, layer):
    p, d = x_sorted.shape
    tm, tf = MOE_TM, MOE_TF
    nf = D_FF // tf
    b3 = b_gu.reshape(N_EXPERTS, 1, 2 * D_FF)
    return pl.pallas_call(
        _moe_up_kernel,
        out_shape=jax.ShapeDtypeStruct((p, D_FF), BF16),
        grid_spec=pltpu.PrefetchScalarGridSpec(
            num_scalar_prefetch=3,
            grid=(nf, p // tm),
            in_specs=[pl.BlockSpec((tm, d), lambda j, i, te, ta, ts: (ts[i], 0)),
                      pl.BlockSpec((None, None, d, tf), lambda j, i, te, ta, ts: (layer, te[i], 0, j)),
                      pl.BlockSpec((None, None, d, tf), lambda j, i, te, ta, ts: (layer, te[i], 0, nf + j)),
                      pl.BlockSpec((None, 1, tf), lambda j, i, te, ta, ts: (te[i], 0, j)),
                      pl.BlockSpec((None, 1, tf), lambda j, i, te, ta, ts: (te[i], 0, nf + j)),
                      pl.BlockSpec((tm, 1), lambda j, i, te, ta, ts: (ts[i], 0))],
            out_specs=pl.BlockSpec((tm, tf), lambda j, i, te, ta, ts: (i, j)),
            scratch_shapes=[pltpu.VMEM((d, tf), BF16), pltpu.VMEM((d, tf), BF16)]),
        compiler_params=_cparams(),
        name="moe_up",
    )(tile_expert, tile_active, tile_src, x_sorted, w_gu, w_gu, b3, b3, row_w)


def _moe_down_kernel(te_ref, ta_ref, ts_ref, a_ref, w_ref, b_ref, rw_ref, o_ref, w_s):
    i = pl.program_id(1)
    changed = jnp.logical_or(i == 0, te_ref[i] != te_ref[jnp.maximum(i - 1, 0)])

    @pl.when(changed)
    def _():
        w_s[...] = _bf(w_ref[...])

    @pl.when(ta_ref[i] == 1)
    def _():
        o_ref[...] = (jnp.dot(a_ref[...], w_s[...], preferred_element_type=F32)
                      + rw_ref[...] * b_ref[...])

    @pl.when(ta_ref[i] == 0)
    def _():
        o_ref[...] = jnp.zeros_like(o_ref)


def _moe_down(act, w_down, b_down, row_w, tile_expert, tile_active, tile_src, layer, tn=1024):
    p, f = act.shape
    d = w_down.shape[-1]
    tm = MOE_TM
    b3 = b_down.reshape(N_EXPERTS, 1, d)
    return pl.pallas_call(
        _moe_down_kernel,
        out_shape=jax.ShapeDtypeStruct((p, d), F32),
        grid_spec=pltpu.PrefetchScalarGridSpec(
            num_scalar_prefetch=3,
            grid=(d // tn, p // tm),
            in_specs=[pl.BlockSpec((tm, f), lambda j, i, te, ta, ts: (ts[i], 0)),
                      pl.BlockSpec((None, None, f, tn), lambda j, i, te, ta, ts: (layer, te[i], 0, j)),
                      pl.BlockSpec((None, 1, tn), lambda j, i, te, ta, ts: (te[i], 0, j)),
                      pl.BlockSpec((tm, 1), lambda j, i, te, ta, ts: (ts[i], 0))],
            out_specs=pl.BlockSpec((tm, tn), lambda j, i, te, ta, ts: (i, j)),
            scratch_shapes=[pltpu.VMEM((f, tn), BF16)]),
        compiler_params=_cparams(),
        name="moe_down",
    )(tile_expert, tile_active, tile_src, act, w_down, b3, row_w)


def _combine_kernel(dest_ref, y_hbm, x_ref, gate_ref, fg_ref, o_ref, buf, sem, *, tm, final_norm):
    def start(r, _):
        for kk in range(TOP_K):
            _row_copy(y_hbm, buf.at[kk], sem, dest_ref[0, r * TOP_K + kk], r).start()
        return 0

    lax.fori_loop(0, tm, start, 0)

    def wait(r, _):
        for kk in range(TOP_K):
            _row_copy(y_hbm, buf.at[kk], sem, dest_ref[0, r * TOP_K + kk], r).wait()
        return 0

    lax.fori_loop(0, tm, wait, 0)
    y = (buf[0] + buf[1]) + (buf[2] + buf[3])
    x = x_ref[...] + gate_ref[...] * y
    if final_norm:
        x = x * lax.rsqrt(jnp.mean(x * x, axis=-1, keepdims=True) + EPS) * fg_ref[...]
    o_ref[...] = x


def _moe_combine(y_sorted, dest, x, mod4, gate_idx, final_g, *, final_norm, n_ctx_rows, rows_per_latent):
    t, d = x.shape
    tm = COMBINE_TM
    row = lambda i: _mod_row(i * tm, n_ctx_rows, rows_per_latent)
    return pl.pallas_call(
        functools.partial(_combine_kernel, tm=tm, final_norm=final_norm),
        out_shape=jax.ShapeDtypeStruct((t, d), F32),
        grid=(t // tm,),
        in_specs=[pl.BlockSpec((None, 1, tm * TOP_K), lambda i: (i, 0, 0), memory_space=pltpu.SMEM),
                  pl.BlockSpec(memory_space=pl.ANY),
                  pl.BlockSpec((tm, d), lambda i: (i, 0)),
                  pl.BlockSpec((None, None, 1, d), lambda i: (row(i), gate_idx, 0, 0)),
                  pl.BlockSpec((1, d), lambda i: (0, 0))],
        out_specs=pl.BlockSpec((tm, d), lambda i: (i, 0)),
        scratch_shapes=[pltpu.VMEM((TOP_K, tm, d), F32), pltpu.SemaphoreType.DMA(())],
        compiler_params=_cparams(),
        name="moe_combine",
    )(dest.reshape(t // tm, 1, tm * TOP_K), y_sorted, x, mod4, final_g.reshape(1, d))


def _moe_plan(top_i, n_tiles):
    t = top_i.shape[0]
    tm = MOE_TM
    e_flat = top_i.reshape(-1)
    onehot = (e_flat[:, None] == jnp.arange(N_EXPERTS, dtype=jnp.int32)[None, :]).astype(jnp.int32)
    csum = jnp.cumsum(onehot, axis=0)
    counts = csum[-1]
    rank = jnp.sum(onehot * csum, axis=1) - 1
    tiles_per = (counts + tm - 1) // tm
    tile_end = jnp.cumsum(tiles_per)
    group_row0 = (tile_end - tiles_per) * tm
    dest = group_row0[e_flat] + rank
    n_used = tile_end[-1]
    tile_ids = jnp.arange(n_tiles, dtype=jnp.int32)
    tile_expert = jnp.searchsorted(tile_end, tile_ids, side="right").astype(jnp.int32)
    tile_active = (tile_ids < n_used).astype(jnp.int32)
    last_expert = jnp.searchsorted(tile_end, n_used - 1, side="right").astype(jnp.int32)
    tile_expert = jnp.where(tile_active == 1, tile_expert, last_expert)
    tile_src = jnp.minimum(tile_ids, n_used - 1)
    return dest.astype(jnp.int32), tile_expert, tile_active, tile_src


def _moe(x, norm_g, mod4, p, l, final_g, *, final_norm, n_ctx_rows, rows_per_latent):
    t = x.shape[0]
    blk = dict(n_ctx_rows=n_ctx_rows, rows_per_latent=rows_per_latent)
    h2, top_i, top_w = _router(x, norm_g, mod4, 3, 4, p["w_router"][l], p["b_router"][l], **blk)
    top_i, top_w = top_i[:, :TOP_K], top_w[:, :TOP_K]
    n_rows = t * TOP_K + N_EXPERTS * MOE_TM
    dest, tile_expert, tile_active, tile_src = _moe_plan(top_i, n_rows // MOE_TM)
    slot_of_row = jnp.full((n_rows,), -1, jnp.int32).at[dest].set(jnp.arange(t * TOP_K, dtype=jnp.int32))
    real = slot_of_row >= 0
    src_token = jnp.where(real, slot_of_row // TOP_K, jnp.arange(n_rows, dtype=jnp.int32) % t)
    row_w = jnp.where(real, top_w.reshape(-1)[jnp.maximum(slot_of_row, 0)], 0.0).reshape(n_rows, 1)
    x_sorted = _moe_gather(h2, src_token, tile_active, n_rows)
    act = _moe_up(x_sorted, p["w_gu"], p["b_gu"][l], row_w, tile_expert, tile_active, tile_src, l)
    y_sorted = _moe_down(act, p["w_down"], p["b_down"][l], row_w, tile_expert, tile_active, tile_src, l)
    return _moe_combine(y_sorted, dest, x, mod4, 5, final_g, final_norm=final_norm, **blk)


def _trunk(x, cvec, p, final_g, caches, *, n_ctx, ctx_len, n_lat, lat_len):
    n_ctx_rows = n_ctx * ctx_len
    blk = dict(n_ctx_rows=n_ctx_rows, rows_per_latent=lat_len)
    rope = _rope_tables(lat_len)
    dft = {}
    for n in (ctx_len, lat_len):
        fwd, inv = _dft_tables(n)
        fwd_hi = _bf(fwd)
        dft[n] = (fwd_hi, _bf(fwd - fwd_hi.astype(F32)), _bf(inv))
    silu_c = jax.nn.silu(cvec)
    outs = []
    for l in range(DEPTH):
        mod = _matmul(silu_c, p["w_mod"], n=N_MOD * D_MODEL, bias=p["b_mod"][l],
                      tm=16, tn=2048, tk=1024, name="modulation", layer=l)
        mod4 = mod.reshape(16, N_MOD, 1, D_MODEL)
        h = _adaln(x, p["norm1_g"][l], mod4, 0, 1, **blk)
        proj_a = _matmul(h, p["w_in"], n=OFF_AB, tm=2048, tn=1024, tk=512, name="in_proj_a", layer=l)
        w_ab = jnp.pad(p["w_in"][l, :, OFF_AB:OFF_REST], ((0, 0), (0, LANES - 4 * H_A)))
        ab = _matmul(h, w_ab, n=LANES, tm=2048, tn=LANES, tk=2048, name="in_proj_ab")
        w_rest = _bf(p["w_in"][l, :, OFF_REST:])
        proj_r = _matmul(h, w_rest, n=N_REST, tm=1024, tn=1536, tk=512, name="in_proj_rest")

        ck, cv, s0f, s0b = caches[l]
        gates = _gdn_gates(ab, p["a_log"][l], p["dt_bias"][l])
        zeros_state = jnp.zeros((n_ctx, H_A, DK_A, DK_A), F32)
        gdn = functools.partial(_gdn, proj_a, gates, p["conv_a"][l], p["onorm_a"][l])
        oa_c, sf_c, sb_c = gdn(zeros_state, zeros_state, n_seq=n_ctx, seq_len=ctx_len, row0=0)
        oa_l, _, _ = gdn(s0f, s0b, n_seq=n_lat, seq_len=lat_len, row0=n_ctx_rows)

        lam_init = 0.8 - 0.6 * math.exp(-0.3 * l)
        attn = functools.partial(_attention, proj_r, p["lam"][l], p["subln_b"][l], lam_init)
        ob_c = attn(n_seq=n_ctx, seq_len=ctx_len, row0=0)
        ob_l = attn(n_seq=n_lat, seq_len=lat_len, row0=n_ctx_rows, rope=rope, cache=(ck, cv))

        oc = []
        for n_seq, n, row0 in ((n_ctx, ctx_len, 0), (n_lat, lat_len, n_ctx_rows)):
            fwd_hi, fwd_lo, inv = dft[n]
            spec = _hyena_filters(n, p["filt_w1"][l], p["filt_b1"][l], p["filt_freq"][l],
                                  p["filt_w2"][l], p["filt_b2"][l], p["filt_w3"][l], fwd_hi, fwd_lo)
            oc.append(_hyena(proj_r, p["conv_c"][l], spec, p["filt_skip"][l], fwd_hi, inv,
                             n_seq=n_seq, seq_len=n, row0=row0))

        o_a = jnp.concatenate([oa_c, oa_l], axis=0)
        o_b = jnp.concatenate([ob_c, ob_l], axis=0)
        o_c = jnp.concatenate(oc, axis=0)
        merged = _merge(o_a, o_b, o_c, p["w_br_a"], p["w_br_b"], p["w_br_c"], proj_r, l)
        x = _matmul_residual(merged, p["w_out"], x, mod4, 2, l, **blk)
        x = _moe(x, p["norm2_g"][l], mod4, p, l, final_g, final_norm=(l == DEPTH - 1), **blk)

        kv = proj_r[:n_ctx_rows, R_KB:R_XC].reshape(n_ctx, ctx_len, 2, H_B, DV_B)
        outs.append((kv[:, :, 0], kv[:, :, 1], sf_c, sb_c))
    return x, outs


def kernel(x_prompt, x_sample, cache_k, cache_v, state_fwd, state_bwd, c, c_ctx, norm1_g, norm2_g, final_g, w_mod, b_mod, w_in, conv_a, a_log, dt_bias, onorm_a, lam, subln_b, conv_c, filt_w1, filt_b1, filt_freq, filt_w2, filt_b2, filt_w3, filt_skip, w_br_a, w_br_b, w_br_c, w_out, w_router, b_router, w_gu, b_gu, w_down, b_down):
    p = dict(norm1_g=norm1_g, norm2_g=norm2_g, w_mod=w_mod, b_mod=b_mod, w_in=w_in, conv_a=conv_a,
             a_log=a_log, dt_bias=dt_bias, onorm_a=onorm_a, lam=lam, subln_b=subln_b, conv_c=conv_c,
             filt_w1=filt_w1, filt_b1=filt_b1, filt_freq=filt_freq, filt_w2=filt_w2,
             filt_b2=filt_b2, filt_w3=filt_w3, filt_skip=filt_skip, w_br_a=w_br_a,
             w_br_b=w_br_b, w_br_c=w_br_c, w_out=w_out, w_router=w_router, b_router=b_router,
             w_gu=w_gu, b_gu=b_gu, w_down=w_down, b_down=b_down)
    n_ctx, ctx_len, d = x_prompt.shape
    n_lat, lat_len, _ = x_sample.shape
    past = cache_k.shape[2]
    x = jnp.concatenate([x_prompt.reshape(n_ctx * ctx_len, d), x_sample.reshape(n_lat * lat_len, d)], axis=0)
    cvec = jnp.concatenate([c_ctx[None, :], c, jnp.zeros((16 - 1 - n_lat, d), F32)], axis=0)
    caches = [(cache_k[:, l].reshape(n_lat, past, W_B), cache_v[:, l].reshape(n_lat, past, W_B),
               state_fwd[:, l], state_bwd[:, l]) for l in range(DEPTH)]
    y, outs = _trunk(x, cvec, p, final_g, caches, n_ctx=n_ctx, ctx_len=ctx_len, n_lat=n_lat, lat_len=lat_len)
    y_prompt = y[:n_ctx * ctx_len].reshape(n_ctx, ctx_len, d)
    y_sample = y[n_ctx * ctx_len:].reshape(n_lat, lat_len, d)
    stack = lambda idx: jnp.stack([o[idx] for o in outs], axis=1)
    return (y_prompt, y_sample, stack(0), stack(1), stack(2), stack(3))
```

```python
import functools
import math

import jax
import jax.numpy as jnp
from jax import lax
from jax.experimental import pallas as pl
from jax.experimental.pallas import tpu as pltpu

F32 = jnp.float32
BF16 = jnp.bfloat16

D_MODEL = 4096
DEPTH = 2
GRID_W = 64
H_A = 12
DK_A = 128
W_A = H_A * DK_A
SHORT_CONV = 3
H_B = 6
DK_B = 128
DV_B = 2 * DK_B
W_B = H_B * DV_B
ROPE_BASE = 10000.0
C_CH = 1024
HYENA_ORDER = 2
FILT_BANDS = 16
FILT_EMB = 1 + 2 * FILT_BANDS
FILT_HIDDEN = 64
HYENA_FAST_DECAY = 0.3
HYENA_SLOW_DECAY = 1.5
HYENA_TARGET = 1e-2
N_EXPERTS = 32
TOP_K = 4
D_FF = 1024
SWIGLU_LIMIT = 7.0
SWIGLU_ALPHA = 1.702
N_MOD = 6
EPS = 1e-6

OFF_AB = 4 * W_A
OFF_REST = OFF_AB + 4 * H_A
R_QB, R_KB, R_VB = 0, W_B, 2 * W_B
R_XC = 3 * W_B
R_GATES = R_XC + 3 * C_CH
N_REST = R_GATES + 3 * D_MODEL

LANES = 128
VMEM_LIMIT = 56 * 1024 * 1024
GDN_CHUNK = 256
GDN_HEADS = 2
ATT_QBLOCK = 256
MOE_TM = 512
MOE_TF = 512
COMBINE_TM = 128


def _cparams(**kw):
    return pltpu.CompilerParams(vmem_limit_bytes=VMEM_LIMIT, **kw)


def _bf(x):
    return x.astype(BF16)


def _dot(a, b):
    return jnp.dot(_bf(a), _bf(b), preferred_element_type=F32)


def _dot_nt(a, b):
    return lax.dot_general(_bf(a), _bf(b), (((1,), (1,)), ((), ())), preferred_element_type=F32)


def _dot_tn(a, b):
    return lax.dot_general(_bf(a), _bf(b), (((0,), (0,)), ((), ())), preferred_element_type=F32)


def _split2(x):
    hi = _bf(x)
    lo = _bf(x - hi.astype(F32))
    return hi, lo


def _split3(x):
    hi = _bf(x)
    r = x - hi.astype(F32)
    mid = _bf(r)
    lo = _bf(r - mid.astype(F32))
    return hi, mid, lo


def _dot_hl(a, b):
    ah, al = _split2(a)
    bh, bl = _split2(b)
    d = functools.partial(jnp.dot, preferred_element_type=F32)
    return d(ah, bh) + (d(ah, bl) + d(al, bh))


def _dot_exact_lhs(m01, x):
    m = _bf(m01)
    h, mi, lo = _split3(x)
    d = functools.partial(jnp.dot, preferred_element_type=F32)
    return d(m, h) + (d(m, mi) + d(m, lo))


def _sigmoid(x):
    return 1.0 / (1.0 + jnp.exp(-x))


def _silu(x):
    return x * _sigmoid(x)


def _softplus(x):
    return jnp.maximum(x, 0.0) + jnp.log(1.0 + jnp.exp(-jnp.abs(x)))


def _mod_row(row_start, n_ctx_rows, rows_per_latent):
    return jnp.where(row_start < n_ctx_rows, 0, 1 + (row_start - n_ctx_rows) // rows_per_latent)


def _mm_kernel(x_ref, w_ref, o_ref, acc_ref, *, nk):
    k = pl.program_id(2)

    @pl.when(k == 0)
    def _():
        acc_ref[...] = jnp.zeros_like(acc_ref)

    acc_ref[...] += _dot(x_ref[...], w_ref[...])

    @pl.when(k == nk - 1)
    def _():
        o_ref[...] = acc_ref[...].astype(o_ref.dtype)


def _mm_bias_kernel(x_ref, w_ref, b_ref, o_ref, acc_ref, *, nk):
    k = pl.program_id(2)

    @pl.when(k == 0)
    def _():
        acc_ref[...] = jnp.zeros_like(acc_ref)

    acc_ref[...] += _dot(x_ref[...], w_ref[...])

    @pl.when(k == nk - 1)
    def _():
        o_ref[...] = (acc_ref[...] + b_ref[...]).astype(o_ref.dtype)


def _layer_spec(block, index_map, layer):
    if layer is None:
        return pl.BlockSpec(block, index_map)
    return pl.BlockSpec((None,) + tuple(block), lambda *a: (layer,) + tuple(index_map(*a)))


def _matmul(x, w, *, n, col0=0, bias=None, tm, tn, tk, out_dtype=F32, name, layer=None):
    m, kdim = x.shape
    assert m % tm == 0 and n % tn == 0 and kdim % tk == 0 and col0 % tn == 0
    nk = kdim // tk
    cb = col0 // tn
    in_specs = [pl.BlockSpec((tm, tk), lambda i, j, k: (i, k)),
                _layer_spec((tk, tn), lambda i, j, k: (k, j + cb), layer)]
    args = [x, w]
    if bias is None:
        body = functools.partial(_mm_kernel, nk=nk)
    else:
        body = functools.partial(_mm_bias_kernel, nk=nk)
        in_specs.append(pl.BlockSpec((1, tn), lambda i, j, k: (0, j)))
        args.append(bias.reshape(1, n))
    return pl.pallas_call(
        body,
        out_shape=jax.ShapeDtypeStruct((m, n), out_dtype),
        grid=(m // tm, n // tn, nk),
        in_specs=in_specs,
        out_specs=pl.BlockSpec((tm, tn), lambda i, j, k: (i, j)),
        scratch_shapes=[pltpu.VMEM((tm, tn), F32)],
        compiler_params=_cparams(),
        name=name,
    )(*args)


def _mm_nt_kernel(x_ref, w_ref, o_ref, w_bf):
    @pl.when(pl.program_id(1) == 0)
    def _():
        w_bf[...] = _bf(w_ref[0])

    o_ref[...] = lax.dot_general(x_ref[...], w_bf[...], (((1,), (1,)), ((), ())),
                                 preferred_element_type=F32).astype(o_ref.dtype)


def _matmul_nt(x, w_t, *, n, row0, layer, tm, tn, name, out_dtype=F32):
    m, kdim = x.shape
    assert m % tm == 0 and n % tn == 0 and row0 % 8 == 0 and tn % 8 == 0
    assert row0 + n <= w_t.shape[1] and x.dtype == BF16
    return pl.pallas_call(
        _mm_nt_kernel,
        out_shape=jax.ShapeDtypeStruct((m, n), out_dtype),
        grid=(n // tn, m // tm),
        in_specs=[pl.BlockSpec((tm, kdim), lambda j, i: (i, 0)),
                  pl.BlockSpec((pl.Element(1), pl.Element(tn), pl.Element(kdim)),
                               lambda j, i: (layer, (row0 // 8 + j * (tn // 8)) * 8, 0))],
        out_specs=pl.BlockSpec((tm, tn), lambda j, i: (i, j)),
        scratch_shapes=[pltpu.VMEM((tn, kdim), BF16)],
        compiler_params=_cparams(),
        name=name,
    )(x, w_t)


def _adaln_kernel(x_ref, g_ref, scale_ref, shift_ref, o_ref):
    x = x_ref[...]
    y = x * lax.rsqrt(jnp.mean(x * x, axis=-1, keepdims=True) + EPS)
    o_ref[...] = (y * g_ref[...] * (1.0 + scale_ref[...]) + shift_ref[...]).astype(o_ref.dtype)


def _adaln(x, g, mod4, shift_idx, scale_idx, *, n_ctx_rows, rows_per_latent, tm=256):
    t, d = x.shape
    row = lambda i: _mod_row(i * tm, n_ctx_rows, rows_per_latent)
    return pl.pallas_call(
        _adaln_kernel,
        out_shape=jax.ShapeDtypeStruct((t, d), BF16),
        grid=(t // tm,),
        in_specs=[pl.BlockSpec((tm, d), lambda i: (i, 0)),
                  pl.BlockSpec((1, d), lambda i: (0, 0)),
                  pl.BlockSpec((None, None, 1, d), lambda i: (row(i), scale_idx, 0, 0)),
                  pl.BlockSpec((None, None, 1, d), lambda i: (row(i), shift_idx, 0, 0))],
        out_specs=pl.BlockSpec((tm, d), lambda i: (i, 0)),
        compiler_params=_cparams(),
        name="adaln",
    )(x, g.reshape(1, d), mod4, mod4)


def _gdn_gate_kernel(ab_ref, alog_ref, dtb_ref, o_ref):
    ab = ab_ref[...]
    tm = ab.shape[0]
    lane = lax.broadcasted_iota(jnp.int32, ab.shape, 1)
    rows = lax.broadcasted_iota(jnp.int32, (tm, tm), 0)
    cols = lax.broadcasted_iota(jnp.int32, (tm, tm), 1)
    g = -jnp.exp(alog_ref[...]) * _softplus(ab + dtb_ref[...])
    prefix = _dot_exact_lhs((cols <= rows).astype(F32), g)
    suffix = _dot_exact_lhs((cols >= rows).astype(F32), g)
    gcum = jnp.where(lane < H_A, prefix, suffix)
    o_ref[...] = jnp.where(lane < 2 * H_A, gcum, _sigmoid(ab))


def _gdn_gates(ab, a_log, dt_bias):
    t = ab.shape[0]
    tm = GDN_CHUNK
    pad = lambda v: jnp.pad(v.reshape(1, 2 * H_A), ((0, 0), (0, LANES - 2 * H_A)))
    return pl.pallas_call(
        _gdn_gate_kernel,
        out_shape=jax.ShapeDtypeStruct((t, LANES), F32),
        grid=(t // tm,),
        in_specs=[pl.BlockSpec((tm, LANES), lambda i: (i, 0)),
                  pl.BlockSpec((1, LANES), lambda i: (0, 0)),
                  pl.BlockSpec((1, LANES), lambda i: (0, 0))],
        out_specs=pl.BlockSpec((tm, LANES), lambda i: (i, 0)),
        compiler_params=_cparams(),
        name="gdn_gates",
    )(ab, pad(a_log), pad(dt_bias))


def _unit_tri_inverse(mats, rows, cols, dot):
    n = mats[0].shape[0]
    eye = (rows == cols).astype(F32)
    same = lambda s: (rows // s) == (cols // s)
    dps = [jnp.where(same(16), a, 0.0) for a in mats]
    ts = [eye - d for d in dps]
    for _ in range(3):
        dbs = [_bf(d) for d in dps]
        dps = [dot(d, d) for d in dbs]
        ts = [t + dot(t, d) for t, d in zip(ts, dps)]
    s = 16
    while s < n:
        mask = same(2 * s) & jnp.logical_not(same(s))
        tbs = [_bf(t) for t in ts]
        lts = [dot(jnp.where(mask, a, 0.0), tb) for a, tb in zip(mats, tbs)]
        ts = [t - dot(tb, lt) for t, tb, lt in zip(ts, tbs, lts)]
        s *= 2
    return ts


def _gdn_kernel(q_ref, k_ref, v_ref, z_ref, gate_ref, cw_ref, og_ref, s0f_ref, s0b_ref,
                o_ref, sf_ref, sb_ref, qs, ks, vs, of_s, ob_s, *, seq_len, chunk, heads):
    h0 = pl.program_id(1) * heads
    n_chunks = seq_len // chunk
    width = heads * DK_A
    head = lambda j: slice(j * DK_A, (j + 1) * DK_A)

    pos = lax.broadcasted_iota(jnp.int32, (seq_len, width), 0)

    def conv_silu(x_ref, w):
        x = x_ref[...]
        prev = jnp.where(pos == 0, 0.0, pltpu.roll(x, 1, 0))
        nxt = jnp.where(pos == seq_len - 1, 0.0, pltpu.roll(x, seq_len - 1, 0))
        return _silu(prev * w[0:1, :] + x * w[1:2, :] + nxt * w[2:3, :])

    def l2n(x):
        return x * lax.rsqrt(jnp.sum(x * x, axis=-1, keepdims=True) + 1e-6)

    qc = conv_silu(q_ref, cw_ref[0])
    kc = conv_silu(k_ref, cw_ref[1])
    for j in range(heads):
        qs[:, head(j)] = l2n(qc[:, head(j)]) * (DK_A ** -0.5)
        ks[:, head(j)] = l2n(kc[:, head(j)])
    vs[...] = conv_silu(v_ref, cw_ref[2])

    rows = lax.broadcasted_iota(jnp.int32, (chunk, chunk), 0)
    cols = lax.broadcasted_iota(jnp.int32, (chunk, chunk), 1)
    lane = lax.broadcasted_iota(jnp.int32, (chunk, LANES), 1)

    chains = [(j, d) for j in range(heads) for d in (0, 1)]
    masks = {0: (cols <= rows, cols < rows),
             1: (cols >= rows, cols > rows)}
    out_refs = {0: of_s, 1: ob_s}

    def body(i, states):
        r0s = {0: pl.multiple_of(i * chunk, chunk), 1: pl.multiple_of((n_chunks - 1 - i) * chunk, chunk)}
        gates = {d: gate_ref[pl.ds(r0s[d], chunk), :] for d in (0, 1)}
        pick = lambda d, idx: jnp.sum(jnp.where(lane == idx, gates[d], 0.0), axis=-1, keepdims=True)
        qkv = [tuple(s[pl.ds(r0s[d], chunk), head(j)] for s in (qs, ks, vs)) for j, d in chains]
        gcum = [jnp.broadcast_to(pick(d, d * H_A + h0 + j), (chunk, LANES)) for j, d in chains]
        beta = [pick(d, (2 + d) * H_A + h0 + j) for j, d in chains]
        last = {0: chunk - 1, 1: 0}
        g_last = [gcm[last[d]:last[d] + 1, :] for (j, d), gcm in zip(chains, gcum)]
        decay = []
        for (j, d), gcm in zip(chains, gcum):
            gc = jnp.concatenate([gcm] * (chunk // LANES), axis=1)
            incl = masks[d][0]
            decay.append(jnp.where(incl, jnp.exp(jnp.where(incl, gc - gc.T, 0.0)), 0.0))
        kk = [_dot_nt(k, k) for q, k, v in qkv]
        a = [jnp.where(masks[d][1], b * x * dc, 0.0) for (j, d), b, x, dc in zip(chains, beta, kk, decay)]
        t = _unit_tri_inverse(a, rows, cols, _dot)
        rhs = [jnp.concatenate([v * b, k * b * jnp.exp(gcm)], axis=-1)
               for (q, k, v), b, gcm in zip(qkv, beta, gcum)]
        uw = [_dot(ti, r) for ti, r in zip(t, rhs)]
        qk = [_dot_nt(q, k) * dc for (q, k, v), dc in zip(qkv, decay)]
        ws = [_dot(jnp.concatenate([x[:, DK_A:], q * jnp.exp(gcm)], axis=0), s)
              for x, (q, k, v), gcm, s in zip(uw, qkv, gcum, states)]
        v_new = [x[:, :DK_A] - y[:chunk] for x, y in zip(uw, ws)]
        o = [y[chunk:] + _dot(m, vn) for y, m, vn in zip(ws, qk, v_new)]
        for (j, d), oi in zip(chains, o):
            out_refs[d][pl.ds(r0s[d], chunk), head(j)] = oi
        return tuple(s * jnp.exp(gl[:, 0:1]) + _dot_tn(k * jnp.exp(gl - gcm), vn)
                     for s, gl, (q, k, v), gcm, vn in zip(states, g_last, qkv, gcum, v_new))

    init = tuple((s0f_ref, s0b_ref)[d][j] for j, d in chains)
    final = lax.fori_loop(0, n_chunks, body, init)
    for (j, d), s in zip(chains, final):
        (sf_ref, sb_ref)[d][j] = s
    for j in range(heads):
        o = of_s[:, head(j)] + ob_s[:, head(j)]
        o = o * lax.rsqrt(jnp.mean(o * o, axis=-1, keepdims=True) + EPS) * og_ref[...]
        o_ref[:, head(j)] = (o * _silu(z_ref[:, head(j)])).astype(o_ref.dtype)


def _gdn(proj_a, gates, conv_w, onorm_g, s0f, s0b, *, n_seq, seq_len, row0, heads=GDN_HEADS):
    assert row0 % seq_len == 0 and seq_len % GDN_CHUNK == 0 and H_A % heads == 0
    rb = row0 // seq_len
    ng = H_A // heads
    width = heads * DK_A
    cw = conv_w.reshape(SHORT_CONV, 3, W_A).transpose(1, 0, 2)
    tok = lambda part: pl.BlockSpec((seq_len, width), lambda b, h: (b + rb, part * ng + h))
    st = pl.BlockSpec((None, heads, DK_A, DK_A), lambda b, h: (b, h, 0, 0))
    body = functools.partial(_gdn_kernel, seq_len=seq_len, chunk=GDN_CHUNK, heads=heads)
    return pl.pallas_call(
        body,
        out_shape=(jax.ShapeDtypeStruct((n_seq * seq_len, W_A), BF16),
                   jax.ShapeDtypeStruct((n_seq, H_A, DK_A, DK_A), F32),
                   jax.ShapeDtypeStruct((n_seq, H_A, DK_A, DK_A), F32)),
        grid=(n_seq, ng),
        in_specs=[tok(0), tok(1), tok(2), tok(3),
                  pl.BlockSpec((seq_len, LANES), lambda b, h: (b + rb, 0)),
                  pl.BlockSpec((3, SHORT_CONV, width), lambda b, h: (0, 0, h)),
                  pl.BlockSpec((1, DK_A), lambda b, h: (0, 0)),
                  st, st],
        out_specs=(pl.BlockSpec((seq_len, width), lambda b, h: (b, h)), st, st),
        scratch_shapes=[pltpu.VMEM((seq_len, width), F32) for _ in range(5)],
        compiler_params=_cparams(),
        name=f"gdn_L{seq_len}",
    )(proj_a, proj_a, proj_a, proj_a, gates, cw, onorm_g.reshape(1, DK_A), s0f, s0b)


def _rope(x, cos, sin_signed, lane):
    rot = jnp.where((lane % 64) < 32, pltpu.roll(x, LANES - 32, 1), pltpu.roll(x, 32, 1))
    return x * cos + rot * sin_signed


def _attn_kernel(*refs, seq_len, qblock, use_rope, n_cache, lam_init):
    it = iter(refs)
    q_ref, k_ref, v_ref, lam_ref, g_ref = next(it), next(it), next(it), next(it), next(it)
    cos_ref = sin_ref = ck_ref = cv_ref = None
    if use_rope:
        cos_ref, sin_ref = next(it), next(it)
    if n_cache:
        ck_ref, cv_ref = next(it), next(it)
    o_ref = next(it)
    ks = next(it)

    lam = lam_ref[...]
    lam_full = (jnp.exp(jnp.sum(lam[0:1] * lam[1:2], axis=-1, keepdims=True))
                - jnp.exp(jnp.sum(lam[2:3] * lam[3:4], axis=-1, keepdims=True)) + lam_init)
    scale = DK_B ** -0.5
    lane = lax.broadcasted_iota(jnp.int32, (seq_len, DK_B), 1) if use_rope else None
    lane_q = lax.broadcasted_iota(jnp.int32, (qblock, DK_B), 1) if use_rope else None
    for r in range(2):
        kr = k_ref[:, r * DK_B:(r + 1) * DK_B]
        if use_rope:
            kr = _rope(kr, cos_ref[...], sin_ref[...], lane)
        ks[r] = _bf(kr)
    v = _bf(v_ref[...])
    for qb in range(seq_len // qblock):
        sl = slice(qb * qblock, (qb + 1) * qblock)
        probs = []
        for r in range(2):
            qr = q_ref[sl, r * DK_B:(r + 1) * DK_B]
            if use_rope:
                qr = _rope(qr, cos_ref[sl, :], sin_ref[sl, :], lane_q)
            s = _dot_nt(qr, ks[r]) * scale
            m = jnp.max(s, axis=-1, keepdims=True)
            if n_cache:
                sc = _dot_nt(qr, ck_ref[:, r * DK_B:(r + 1) * DK_B]) * scale
                m = jnp.maximum(m, jnp.max(sc, axis=-1, keepdims=True))
                ec = jnp.exp(sc - m)
            e = jnp.exp(s - m)
            den = jnp.sum(e, axis=-1, keepdims=True)
            if n_cache:
                den = den + jnp.sum(ec, axis=-1, keepdims=True)
                probs.append((e / den, ec / den))
            else:
                probs.append((e / den, None))
        o = _dot(probs[0][0] - lam_full * probs[1][0], v)
        if n_cache:
            o = o + _dot(probs[0][1] - lam_full * probs[1][1], cv_ref[...])
        o = o * lax.rsqrt(jnp.mean(o * o, axis=-1, keepdims=True) + 1e-5) * g_ref[...]
        o_ref[sl, :] = (o * (1.0 - lam_init)).astype(o_ref.dtype)


def _attention(proj_r, lam, subln_g, lam_init, *, n_seq, seq_len, row0, rope=None, cache=None):
    assert row0 % seq_len == 0
    rb = row0 // seq_len
    nh = H_B
    blk = lambda part: pl.BlockSpec((seq_len, DV_B), lambda b, h: (b + rb, part * nh + h))
    in_specs = [blk(0), blk(1), blk(2),
                pl.BlockSpec((4, DK_B), lambda b, h: (0, 0)),
                pl.BlockSpec((1, DV_B), lambda b, h: (0, 0))]
    args = [proj_r, proj_r, proj_r, lam, subln_g.reshape(1, DV_B)]
    if rope is not None:
        in_specs += [pl.BlockSpec((seq_len, DK_B), lambda b, h: (0, 0))] * 2
        args += list(rope)
    n_cache = 0
    if cache is not None:
        ck, cv = cache
        n_cache = ck.shape[1]
        in_specs += [pl.BlockSpec((None, n_cache, DV_B), lambda b, h: (b, 0, h))] * 2
        args += [ck, cv]
    body = functools.partial(_attn_kernel, seq_len=seq_len, qblock=min(ATT_QBLOCK, seq_len),
                             use_rope=rope is not None, n_cache=n_cache, lam_init=lam_init)
    return pl.pallas_call(
        body,
        out_shape=jax.ShapeDtypeStruct((n_seq * seq_len, W_B), BF16),
        grid=(n_seq, nh),
        in_specs=in_specs,
        out_specs=pl.BlockSpec((seq_len, DV_B), lambda b, h: (b, h)),
        scratch_shapes=[pltpu.VMEM((2, seq_len, DK_B), BF16)],
        compiler_params=_cparams(),
        name=f"diff_attn_L{seq_len}",
    )(*args)


def _rope_tables(n_tok):
    rows = n_tok // GRID_W
    row = jnp.repeat(jnp.arange(rows), GRID_W)
    col = jnp.tile(jnp.arange(GRID_W), rows)
    half = DK_B // 2
    inv = ROPE_BASE ** (-jnp.arange(0, half, 2, dtype=F32) / half)
    ang = jnp.stack([row, col], axis=-1).astype(F32)[..., None] * inv
    cos, sin = jnp.cos(ang), jnp.sin(ang)
    cos_t = jnp.concatenate([cos, cos], axis=-1).reshape(n_tok, DK_B)
    sin_t = jnp.concatenate([-sin, sin], axis=-1).reshape(n_tok, DK_B)
    return cos_t, sin_t


def _dft_tables(n):
    f = jnp.arange(n, dtype=jnp.int32)[:, None]
    t = jnp.arange(n, dtype=jnp.int32)[None, :]
    ang = ((f * t) % (2 * n)).astype(F32) * (math.pi / n)
    cos, sin = jnp.cos(ang), jnp.sin(ang)
    nyq = jnp.where(t % 2 == 0, 1.0, -1.0).astype(F32)
    fwd_im = jnp.where(f == 0, nyq, -sin)
    fwd = jnp.concatenate([cos, fwd_im], axis=0)
    wgt = jnp.where(f == 0, 1.0, 2.0).astype(F32) / (2 * n)
    inv_re = (wgt * cos).T
    inv_im = jnp.where(f == 0, nyq / (2 * n), -wgt * sin).T
    inv = jnp.concatenate([inv_re, inv_im], axis=1)
    return fwd, inv


def _spec_mul(u, s, n, row):
    ur, ui, sr, si = u[:n], u[n:], s[:n], s[n:]
    first = row == 0
    yr = ur * sr - jnp.where(first, 0.0, ui * si)
    yi = jnp.where(first, ui * si, ur * si + ui * sr)
    return yr, yi


def _hyena_filter_kernel(z_ref, w1_ref, b1_ref, fr_ref, w2_ref, b2_ref, w3_ref, win_ref,
                         fh_ref, fl_ref, o_ref, *, n):
    fr = fr_ref[...]
    h = jnp.sin(fr * (_dot_hl(z_ref[...], w1_ref[...]) + b1_ref[...]))
    h = jnp.sin(fr * (_dot_hl(h, w2_ref[...]) + b2_ref[...]))
    win = win_ref[...]
    row = lax.broadcasted_iota(jnp.int32, win.shape, 0)
    fwd_hi, fwd_lo = fh_ref[...], fl_ref[...]

    def dft(x):
        xh, xl = _split2(x)
        d = functools.partial(jnp.dot, preferred_element_type=F32)
        return d(fwd_hi, xh) + (d(fwd_hi, xl) + d(fwd_lo, xh))

    for o in range(HYENA_ORDER):
        hf = _dot_hl(h, w3_ref[2 * o]) * win
        hb = jnp.where(row == 0, 0.0, _dot_hl(h, w3_ref[2 * o + 1]) * win)
        p, q = dft(hf), dft(hb)
        o_ref[o, :n, :] = p[:n] + q[:n]
        o_ref[o, n:, :] = jnp.where(row == 0, p[n:] + q[n:], p[n:] - q[n:])


def _hyena_filters(n, w1, b1, freq, w2, b2, w3, fwd_hi, fwd_lo, tc=256):
    t = jnp.linspace(0.0, 1.0, n, dtype=F32)[:, None]
    wpos = 2.0 * math.pi * jnp.arange(n, dtype=F32)[:, None] / n
    f = jnp.linspace(1e-4, FILT_BANDS - 1, FILT_BANDS, dtype=F32)
    z = jnp.concatenate([t, jnp.cos(wpos * f), -jnp.sin(wpos * f)], axis=-1)
    z = jnp.pad(z, ((0, 0), (0, LANES - FILT_EMB)))
    w1p = jnp.pad(w1, ((0, LANES - FILT_EMB), (0, 0)))
    max_decay = math.log(HYENA_TARGET) / HYENA_FAST_DECAY
    min_decay = math.log(HYENA_TARGET) / HYENA_SLOW_DECAY
    deltas = jnp.linspace(min_decay, max_decay, C_CH, dtype=F32)
    window = jnp.exp(-t * jnp.abs(deltas))
    w3r = w3.reshape(FILT_HIDDEN, 2 * HYENA_ORDER, C_CH).transpose(1, 0, 2)
    full = lambda shape: pl.BlockSpec(shape, lambda j: (0,) * len(shape))
    return pl.pallas_call(
        functools.partial(_hyena_filter_kernel, n=n),
        out_shape=jax.ShapeDtypeStruct((HYENA_ORDER, 2 * n, C_CH), F32),
        grid=(C_CH // tc,),
        in_specs=[full((n, LANES)), full((LANES, FILT_HIDDEN)), full((1, FILT_HIDDEN)),
                  full((1, FILT_HIDDEN)), full((FILT_HIDDEN, FILT_HIDDEN)), full((1, FILT_HIDDEN)),
                  pl.BlockSpec((2 * HYENA_ORDER, FILT_HIDDEN, tc), lambda j: (0, 0, j)),
                  pl.BlockSpec((n, tc), lambda j: (0, j)),
                  full((2 * n, n)), full((2 * n, n))],
        out_specs=pl.BlockSpec((HYENA_ORDER, 2 * n, tc), lambda j: (0, 0, j)),
        compiler_params=_cparams(),
        name=f"hyena_filter_L{n}",
    )(z, w1p, b1.reshape(1, -1), freq.reshape(1, -1), w2, b2.reshape(1, -1), w3r, window,
      fwd_hi, fwd_lo)


def _hyena_kernel(x1_ref, x2_ref, v_ref, cw_ref, spec_ref, skip_ref, fwd_ref, inv_ref, o_ref, *, n):
    shape = v_ref.shape
    row = lax.broadcasted_iota(jnp.int32, shape, 0)

    def conv3(x_ref, p):
        x = x_ref[...]
        prev = jnp.where(row == 0, 0.0, pltpu.roll(x, 1, 0))
        nxt = jnp.where(row == n - 1, 0.0, pltpu.roll(x, n - 1, 0))
        return prev * cw_ref[p, 0:1, :] + x * cw_ref[p, 1:2, :] + nxt * cw_ref[p, 2:3, :]

    z = conv3(v_ref, 2)
    fwd, inv = fwd_ref[...], inv_ref[...]
    for o, gate_ref in enumerate((x1_ref, x2_ref)):
        u = jnp.dot(fwd, _bf(z), preferred_element_type=F32)
        yr, yi = _spec_mul(u, spec_ref[o], n, row)
        y = jnp.dot(inv, _bf(jnp.concatenate([yr, yi], axis=0)), preferred_element_type=F32)
        z = conv3(gate_ref, o) * (y + z * skip_ref[o:o + 1, :])
    o_ref[...] = z.astype(o_ref.dtype)


def _hyena(proj_r, conv_w, spec, skip, fwd, inv, *, n_seq, seq_len, row0, tc=256):
    assert row0 % seq_len == 0
    rb = row0 // seq_len
    xb = lambda part: pl.BlockSpec((seq_len, tc), lambda b, j: (b + rb, (R_XC + part * C_CH) // tc + j))
    cw = conv_w.reshape(SHORT_CONV, 3, C_CH).transpose(1, 0, 2)
    return pl.pallas_call(
        functools.partial(_hyena_kernel, n=seq_len),
        out_shape=jax.ShapeDtypeStruct((n_seq * seq_len, C_CH), BF16),
        grid=(n_seq, C_CH // tc),
        in_specs=[xb(0), xb(1), xb(2),
                  pl.BlockSpec((3, SHORT_CONV, tc), lambda b, j: (0, 0, j)),
                  pl.BlockSpec((HYENA_ORDER, 2 * seq_len, tc), lambda b, j: (0, 0, j)),
                  pl.BlockSpec((HYENA_ORDER, tc), lambda b, j: (0, j)),
                  pl.BlockSpec((2 * seq_len, seq_len), lambda b, j: (0, 0)),
                  pl.BlockSpec((seq_len, 2 * seq_len), lambda b, j: (0, 0))],
        out_specs=pl.BlockSpec((seq_len, tc), lambda b, j: (b, j)),
        compiler_params=_cparams(),
        name=f"hyena_L{seq_len}",
    )(proj_r, proj_r, proj_r, cw, spec, skip, fwd, inv)


def _merge_kernel(oa_ref, ob_ref, oc_ref, wa_ref, wb_ref, wc_ref, ga_ref, gb_ref, gc_ref, o_ref,
                  wa_bf, wb_bf, wc_bf):
    @pl.when(pl.program_id(1) == 0)
    def _():
        wa_bf[...] = _bf(wa_ref[...])
        wb_bf[...] = _bf(wb_ref[...])
        wc_bf[...] = _bf(wc_ref[...])

    d = functools.partial(jnp.dot, preferred_element_type=F32)
    acc = _sigmoid(ga_ref[...]) * d(oa_ref[...], wa_bf[...])
    acc += _sigmoid(gb_ref[...]) * d(ob_ref[...], wb_bf[...])
    acc += _sigmoid(gc_ref[...]) * d(oc_ref[...], wc_bf[...])
    o_ref[...] = acc.astype(o_ref.dtype)


def _merge(o_a, o_b, o_c, w_a, w_b, w_c, proj_r, layer, tm=512, tn=512):
    t = o_a.shape[0]
    d = D_MODEL
    gate = lambda part: pl.BlockSpec((tm, tn), lambda j, i: (i, (R_GATES + part * d) // tn + j))
    act = lambda width: pl.BlockSpec((tm, width), lambda j, i: (i, 0))
    wgt = lambda width: _layer_spec((width, tn), lambda j, i: (0, j), layer)
    return pl.pallas_call(
        _merge_kernel,
        out_shape=jax.ShapeDtypeStruct((t, d), BF16),
        grid=(d // tn, t // tm),
        in_specs=[act(W_A), act(W_B), act(C_CH), wgt(W_A), wgt(W_B), wgt(C_CH),
                  gate(0), gate(1), gate(2)],
        out_specs=pl.BlockSpec((tm, tn), lambda j, i: (i, j)),
        scratch_shapes=[pltpu.VMEM((W_A, tn), BF16), pltpu.VMEM((W_B, tn), BF16), pltpu.VMEM((C_CH, tn), BF16)],
        compiler_params=_cparams(),
        name="merge",
    )(o_a, o_b, o_c, w_a, w_b, w_c, proj_r, proj_r, proj_r)


def _mm_resid_kernel(y_ref, w_ref, x_ref, gate_ref, o_ref, w_bf):
    @pl.when(pl.program_id(1) == 0)
    def _():
        w_bf[...] = _bf(w_ref[...])

    o_ref[...] = x_ref[...] + gate_ref[...] * jnp.dot(y_ref[...], w_bf[...], preferred_element_type=F32)


def _matmul_residual(y, w, x, mod4, gate_idx, layer, *, n_ctx_rows, rows_per_latent, tm=512, tn=512):
    t, kdim = y.shape
    n = w.shape[-1]
    assert y.dtype == BF16
    row = lambda i: _mod_row(i * tm, n_ctx_rows, rows_per_latent)
    return pl.pallas_call(
        _mm_resid_kernel,
        out_shape=jax.ShapeDtypeStruct((t, n), F32),
        grid=(n // tn, t // tm),
        in_specs=[pl.BlockSpec((tm, kdim), lambda j, i: (i, 0)),
                  _layer_spec((kdim, tn), lambda j, i: (0, j), layer),
                  pl.BlockSpec((tm, tn), lambda j, i: (i, j)),
                  pl.BlockSpec((None, None, 1, tn), lambda j, i: (row(i), gate_idx, 0, j))],
        out_specs=pl.BlockSpec((tm, tn), lambda j, i: (i, j)),
        scratch_shapes=[pltpu.VMEM((kdim, tn), BF16)],
        compiler_params=_cparams(),
        name="out_proj_residual",
    )(y, w, x, mod4)


def _router_kernel(x_ref, g_ref, scale_ref, shift_ref, wr_ref, br_ref, h_ref, ti_ref, tw_ref):
    x = x_ref[...]
    y = x * lax.rsqrt(jnp.mean(x * x, axis=-1, keepdims=True) + EPS)
    h = y * g_ref[...] * (1.0 + scale_ref[...]) + shift_ref[...]
    half = h.shape[1] // 2
    hb = _bf(h).astype(F32)
    lo = lax.bitcast_convert_type(hb[:, :half], jnp.uint32)
    hi = lax.bitcast_convert_type(hb[:, half:], jnp.uint32)
    h_ref[...] = hi | (lo >> 16)
    logits = _dot_hl(h, wr_ref[...]) + br_ref[...]
    lane_i = lax.broadcasted_iota(jnp.int32, logits.shape, 1)
    lane = lane_i.astype(F32)
    neg = jnp.float32(-jnp.inf)
    cur = jnp.where(lane_i < N_EXPERTS, logits, neg)
    vals = []
    ti = jnp.zeros(logits.shape, F32)
    for kk in range(TOP_K):
        m = jnp.max(cur, axis=-1, keepdims=True)
        idx = jnp.min(jnp.where(cur == m, lane, float(LANES)), axis=-1, keepdims=True)
        ti = jnp.where(lane_i == kk, idx, ti)
        vals.append(m)
        cur = jnp.where(lane == idx, neg, cur)
    es = [jnp.exp(vk - vals[0]) for vk in vals]
    den = es[0] + es[1] + es[2] + es[3]
    tw = jnp.zeros(logits.shape, F32)
    for kk in range(TOP_K):
        tw = jnp.where(lane_i == kk, es[kk] / den, tw)
    ti_ref[...] = ti.astype(jnp.int32)
    tw_ref[...] = tw


def _router(x, g, mod4, shift_idx, scale_idx, w_router, b_router, *, n_ctx_rows, rows_per_latent, tm=256):
    t, d = x.shape
    row = lambda i: _mod_row(i * tm, n_ctx_rows, rows_per_latent)
    wr = jnp.pad(w_router, ((0, 0), (0, LANES - N_EXPERTS)))
    br = jnp.pad(b_router.reshape(1, -1), ((0, 0), (0, LANES - N_EXPERTS)))
    return pl.pallas_call(
        _router_kernel,
        out_shape=(jax.ShapeDtypeStruct((t, d // 2), jnp.uint32),
                   jax.ShapeDtypeStruct((t, LANES), jnp.int32),
                   jax.ShapeDtypeStruct((t, LANES), F32)),
        grid=(t // tm,),
        in_specs=[pl.BlockSpec((tm, d), lambda i: (i, 0)),
                  pl.BlockSpec((1, d), lambda i: (0, 0)),
                  pl.BlockSpec((None, None, 1, d), lambda i: (row(i), scale_idx, 0, 0)),
                  pl.BlockSpec((None, None, 1, d), lambda i: (row(i), shift_idx, 0, 0)),
                  pl.BlockSpec((d, LANES), lambda i: (0, 0)),
                  pl.BlockSpec((1, LANES), lambda i: (0, 0))],
        out_specs=(pl.BlockSpec((tm, d // 2), lambda i: (i, 0)),
                   pl.BlockSpec((tm, LANES), lambda i: (i, 0)),
                   pl.BlockSpec((tm, LANES), lambda i: (i, 0))),
        compiler_params=_cparams(),
        name="router",
    )(x, g.reshape(1, d), mod4, mod4, wr, br)


def _row_copy(src_hbm, dst_vmem, sem, src_row, dst_row):
    return pltpu.make_async_copy(src_hbm.at[pl.ds(src_row, 1), :], dst_vmem.at[pl.ds(dst_row, 1), :], sem)


def _gather_kernel(ta_ref, src_ref, nxt_ref, h_hbm, o_ref, buf, sem, *, tm):
    i = pl.program_id(0)
    last = pl.num_programs(0) - 1
    slot = i % 2

    def fetch(idx_ref, s):
        def start(r, _):
            _row_copy(h_hbm, buf.at[s], sem.at[s], idx_ref[0, r], r).start()
            return 0

        lax.fori_loop(0, tm, start, 0)

    @pl.when(jnp.logical_and(i == 0, ta_ref[0] == 1))
    def _():
        fetch(src_ref, 0)

    @pl.when(jnp.logical_and(i < last, ta_ref[jnp.minimum(i + 1, last)] == 1))
    def _():
        fetch(nxt_ref, 1 - slot)

    @pl.when(ta_ref[i] == 1)
    def _():
        def wait(r, _):
            _row_copy(h_hbm, buf.at[slot], sem.at[slot], src_ref[0, r], r).wait()
            return 0

        lax.fori_loop(0, tm, wait, 0)
        u = buf[slot]
        half = u.shape[1]
        o_ref[:, :half] = lax.bitcast_convert_type(u << 16, F32).astype(o_ref.dtype)
        o_ref[:, half:] = lax.bitcast_convert_type(u & jnp.uint32(0xFFFF0000), F32).astype(o_ref.dtype)

    @pl.when(ta_ref[i] == 0)
    def _():
        o_ref[...] = jnp.zeros_like(o_ref)


def _moe_gather(h_packed, src_token, tile_active, n_rows):
    h = h_packed
    d = 2 * h.shape[1]
    tm = MOE_TM
    n_tiles = n_rows // tm
    src3 = src_token.reshape(n_tiles, 1, tm)
    return pl.pallas_call(
        functools.partial(_gather_kernel, tm=tm),
        out_shape=jax.ShapeDtypeStruct((n_rows, d), BF16),
        grid_spec=pltpu.PrefetchScalarGridSpec(
            num_scalar_prefetch=1,
            grid=(n_tiles,),
            in_specs=[pl.BlockSpec((None, 1, tm), lambda i, ta: (i, 0, 0), memory_space=pltpu.SMEM),
                      pl.BlockSpec((None, 1, tm), lambda i, ta: (jnp.minimum(i + 1, n_tiles - 1), 0, 0),
                                   memory_space=pltpu.SMEM),
                      pl.BlockSpec(memory_space=pl.ANY)],
            out_specs=pl.BlockSpec((tm, d), lambda i, ta: (i, 0)),
            scratch_shapes=[pltpu.VMEM((2, tm, d // 2), jnp.uint32), pltpu.SemaphoreType.DMA((2,))]),
        compiler_params=_cparams(dimension_semantics=("arbitrary",)),
        name="moe_gather",
    )(tile_active, src3, src3, h)


def _moe_up_kernel(te_ref, ta_ref, ts_ref, x_ref, wg_ref, wu_ref, bg_ref, bu_ref, rw_ref, o_ref):
    i = pl.program_id(1)

    @pl.when(ta_ref[i] == 1)
    def _():
        x = x_ref[...]
        gate = _dot(x, wg_ref[...]) + bg_ref[...]
        up = _dot(x, wu_ref[...]) + bu_ref[...]
        gate = jnp.minimum(gate, SWIGLU_LIMIT)
        up = jnp.clip(up, -SWIGLU_LIMIT, SWIGLU_LIMIT)
        act = (up + 1.0) * gate * _sigmoid(SWIGLU_ALPHA * gate)
        o_ref[...] = (act * rw_ref[...]).astype(o_ref.dtype)

    @pl.when(ta_ref[i] == 0)
    def _():
        o_ref[...] = jnp.zeros_like(o_ref)


def _moe_up(x_sorted, w_gu, b_gu, row_w, tile_expert, tile_active, tile_src, layer):
    p, d = x_sorted.shape
    tm, tf = MOE_TM, MOE_TF
    nf = D_FF // tf
    b3 = b_gu.reshape(N_EXPERTS, 1, 2 * D_FF)
    return pl.pallas_call(
        _moe_up_kernel,
        out_shape=jax.ShapeDtypeStruct((p, D_FF), BF16),
        grid_spec=pltpu.PrefetchScalarGridSpec(
            num_scalar_prefetch=3,
            grid=(nf, p // tm),
            in_specs=[pl.BlockSpec((tm, d), lambda j, i, te, ta, ts: (ts[i], 0)),
                      pl.BlockSpec((None, None, d, tf), lambda j, i, te, ta, ts: (layer, te[i], 0, j)),
                      pl.BlockSpec((None, None, d, tf), lambda j, i, te, ta, ts: (layer, te[i], 0, nf + j)),
                      pl.BlockSpec((None, 1, tf), lambda j, i, te, ta, ts: (te[i], 0, j)),
                      pl.BlockSpec((None, 1, tf), lambda j, i, te, ta, ts: (te[i], 0, nf + j)),
                      pl.BlockSpec((tm, 1), lambda j, i, te, ta, ts: (ts[i], 0))],
            out_specs=pl.BlockSpec((tm, tf), lambda j, i, te, ta, ts: (i, j)),
            scratch_shapes=[]),
        compiler_params=_cparams(),
        name="moe_up",
    )(tile_expert, tile_active, tile_src, x_sorted, w_gu, w_gu, b3, b3, row_w)


def _moe_down_kernel(te_ref, ta_ref, ts_ref, a_ref, w_ref, b_ref, rw_ref, o_ref, w_s):
    i = pl.program_id(1)
    changed = jnp.logical_or(i == 0, te_ref[i] != te_ref[jnp.maximum(i - 1, 0)])

    @pl.when(changed)
    def _():
        w_s[...] = _bf(w_ref[...])

    @pl.when(ta_ref[i] == 1)
    def _():
        o_ref[...] = (jnp.dot(a_ref[...], w_s[...], preferred_element_type=F32)
                      + rw_ref[...] * b_ref[...])

    @pl.when(ta_ref[i] == 0)
    def _():
        o_ref[...] = jnp.zeros_like(o_ref)


def _moe_down(act, w_down, b_down, row_w, tile_expert, tile_active, tile_src, layer, tn=1024):
    p, f = act.shape
    d = w_down.shape[-1]
    tm = MOE_TM
    b3 = b_down.reshape(N_EXPERTS, 1, d)
    return pl.pallas_call(
        _moe_down_kernel,
        out_shape=jax.ShapeDtypeStruct((p, d), F32),
        grid_spec=pltpu.PrefetchScalarGridSpec(
            num_scalar_prefetch=3,
            grid=(d // tn, p // tm),
            in_specs=[pl.BlockSpec((tm, f), lambda j, i, te, ta, ts: (ts[i], 0)),
                      pl.BlockSpec((None, None, f, tn), lambda j, i, te, ta, ts: (layer, te[i], 0, j)),
                      pl.BlockSpec((None, 1, tn), lambda j, i, te, ta, ts: (te[i], 0, j)),
                      pl.BlockSpec((tm, 1), lambda j, i, te, ta, ts: (ts[i], 0))],
            out_specs=pl.BlockSpec((tm, tn), lambda j, i, te, ta, ts: (i, j)),
            scratch_shapes=[pltpu.VMEM((f, tn), BF16)]),
        compiler_params=_cparams(),
        name="moe_down",
    )(tile_expert, tile_active, tile_src, act, w_down, b3, row_w)


def _combine_kernel(dest_ref, nxt_ref, y_hbm, x_ref, gate_ref, fg_ref, o_ref, buf, sem, *, tm, final_norm):
    i = pl.program_id(0)
    last = pl.num_programs(0) - 1
    slot = i % 2

    def fetch(idx_ref, s):
        def start(r, _):
            for kk in range(TOP_K):
                _row_copy(y_hbm, buf.at[s, kk], sem.at[s], idx_ref[0, r * TOP_K + kk], r).start()
            return 0

        lax.fori_loop(0, tm, start, 0)

    @pl.when(i == 0)
    def _():
        fetch(dest_ref, 0)

    @pl.when(i < last)
    def _():
        fetch(nxt_ref, 1 - slot)

    def wait(r, _):
        for kk in range(TOP_K):
            _row_copy(y_hbm, buf.at[slot, kk], sem.at[slot], dest_ref[0, r * TOP_K + kk], r).wait()
        return 0

    lax.fori_loop(0, tm, wait, 0)
    y = (buf[slot, 0] + buf[slot, 1]) + (buf[slot, 2] + buf[slot, 3])
    x = x_ref[...] + gate_ref[...] * y
    if final_norm:
        x = x * lax.rsqrt(jnp.mean(x * x, axis=-1, keepdims=True) + EPS) * fg_ref[...]
    o_ref[...] = x


def _moe_combine(y_sorted, dest, x, mod4, gate_idx, final_g, *, final_norm, n_ctx_rows, rows_per_latent):
    t, d = x.shape
    tm = COMBINE_TM
    row = lambda i: _mod_row(i * tm, n_ctx_rows, rows_per_latent)
    n_tiles = t // tm
    dest3 = dest.reshape(n_tiles, 1, tm * TOP_K)
    return pl.pallas_call(
        functools.partial(_combine_kernel, tm=tm, final_norm=final_norm),
        out_shape=jax.ShapeDtypeStruct((t, d), F32),
        grid=(n_tiles,),
        in_specs=[pl.BlockSpec((None, 1, tm * TOP_K), lambda i: (i, 0, 0), memory_space=pltpu.SMEM),
                  pl.BlockSpec((None, 1, tm * TOP_K), lambda i: (jnp.minimum(i + 1, n_tiles - 1), 0, 0),
                               memory_space=pltpu.SMEM),
                  pl.BlockSpec(memory_space=pl.ANY),
                  pl.BlockSpec((tm, d), lambda i: (i, 0)),
                  pl.BlockSpec((None, None, 1, d), lambda i: (row(i), gate_idx, 0, 0)),
                  pl.BlockSpec((1, d), lambda i: (0, 0))],
        out_specs=pl.BlockSpec((tm, d), lambda i: (i, 0)),
        scratch_shapes=[pltpu.VMEM((2, TOP_K, tm, d), F32), pltpu.SemaphoreType.DMA((2,))],
        compiler_params=_cparams(dimension_semantics=("arbitrary",)),
        name="moe_combine",
    )(dest3, dest3, y_sorted, x, mod4, final_g.reshape(1, d))


def _moe_plan(top_i, n_tiles):
    t = top_i.shape[0]
    tm = MOE_TM
    e_flat = top_i.reshape(-1)
    onehot = (e_flat[:, None] == jnp.arange(N_EXPERTS, dtype=jnp.int32)[None, :]).astype(jnp.int32)
    csum = jnp.cumsum(onehot, axis=0)
    counts = csum[-1]
    rank = jnp.sum(onehot * csum, axis=1) - 1
    tiles_per = (counts + tm - 1) // tm
    tile_end = jnp.cumsum(tiles_per)
    group_row0 = (tile_end - tiles_per) * tm
    dest = group_row0[e_flat] + rank
    n_used = tile_end[-1]
    tile_ids = jnp.arange(n_tiles, dtype=jnp.int32)
    tile_expert = jnp.searchsorted(tile_end, tile_ids, side="right").astype(jnp.int32)
    tile_active = (tile_ids < n_used).astype(jnp.int32)
    last_expert = jnp.searchsorted(tile_end, n_used - 1, side="right").astype(jnp.int32)
    tile_expert = jnp.where(tile_active == 1, tile_expert, last_expert)
    tile_src = jnp.minimum(tile_ids, n_used - 1)
    return dest.astype(jnp.int32), tile_expert, tile_active, tile_src


def _moe(x, norm_g, mod4, p, l, final_g, *, final_norm, n_ctx_rows, rows_per_latent):
    t = x.shape[0]
    blk = dict(n_ctx_rows=n_ctx_rows, rows_per_latent=rows_per_latent)
    h2, top_i, top_w = _router(x, norm_g, mod4, 3, 4, p["w_router"][l], p["b_router"][l], **blk)
    top_i, top_w = top_i[:, :TOP_K], top_w[:, :TOP_K]
    n_rows = t * TOP_K + N_EXPERTS * MOE_TM
    dest, tile_expert, tile_active, tile_src = _moe_plan(top_i, n_rows // MOE_TM)
    slot_of_row = jnp.full((n_rows,), -1, jnp.int32).at[dest].set(jnp.arange(t * TOP_K, dtype=jnp.int32))
    real = slot_of_row >= 0
    src_token = jnp.where(real, slot_of_row // TOP_K, jnp.arange(n_rows, dtype=jnp.int32) % t)
    row_w = jnp.where(real, top_w.reshape(-1)[jnp.maximum(slot_of_row, 0)], 0.0).reshape(n_rows, 1)
    x_sorted = _moe_gather(h2, src_token, tile_active, n_rows)
    act = _moe_up(x_sorted, p["w_gu"], p["b_gu"][l], row_w, tile_expert, tile_active, tile_src, l)
    y_sorted = _moe_down(act, p["w_down"], p["b_down"][l], row_w, tile_expert, tile_active, tile_src, l)
    return _moe_combine(y_sorted, dest, x, mod4, 5, final_g, final_norm=final_norm, **blk)


def _trunk(x, cvec, p, final_g, caches, *, n_ctx, ctx_len, n_lat, lat_len):
    n_ctx_rows = n_ctx * ctx_len
    blk = dict(n_ctx_rows=n_ctx_rows, rows_per_latent=lat_len)
    rope = _rope_tables(lat_len)
    dft = {}
    for n in (ctx_len, lat_len):
        fwd, inv = _dft_tables(n)
        fwd_hi = _bf(fwd)
        dft[n] = (fwd_hi, _bf(fwd - fwd_hi.astype(F32)), _bf(inv))
    silu_c = jax.nn.silu(cvec)
    w_in_t = jnp.swapaxes(p["w_in"], 1, 2)
    outs = []
    for l in range(DEPTH):
        mod = _matmul(silu_c, p["w_mod"], n=N_MOD * D_MODEL, bias=p["b_mod"][l],
                      tm=16, tn=2048, tk=1024, name="modulation", layer=l)
        mod4 = mod.reshape(16, N_MOD, 1, D_MODEL)
        h = _adaln(x, p["norm1_g"][l], mod4, 0, 1, **blk)
        in_proj = functools.partial(_matmul_nt, h, w_in_t, layer=l)
        proj_a = in_proj(n=OFF_AB, row0=0, tm=512, tn=768, name="in_proj_a")
        ab = in_proj(n=LANES, row0=OFF_AB, tm=1024, tn=LANES, name="in_proj_ab")
        proj_r = in_proj(n=N_REST, row0=OFF_REST, tm=512, tn=768, name="in_proj_rest")

        ck, cv, s0f, s0b = caches[l]
        gates = _gdn_gates(ab, p["a_log"][l], p["dt_bias"][l])
        zeros_state = jnp.zeros((n_ctx, H_A, DK_A, DK_A), F32)
        gdn = functools.partial(_gdn, proj_a, gates, p["conv_a"][l], p["onorm_a"][l])
        oa_c, sf_c, sb_c = gdn(zeros_state, zeros_state, n_seq=n_ctx, seq_len=ctx_len, row0=0)
        oa_l, _, _ = gdn(s0f, s0b, n_seq=n_lat, seq_len=lat_len, row0=n_ctx_rows)

        lam_init = 0.8 - 0.6 * math.exp(-0.3 * l)
        attn = functools.partial(_attention, proj_r, p["lam"][l], p["subln_b"][l], lam_init)
        ob_c = attn(n_seq=n_ctx, seq_len=ctx_len, row0=0)
        ob_l = attn(n_seq=n_lat, seq_len=lat_len, row0=n_ctx_rows, rope=rope, cache=(ck, cv))

        oc = []
        for n_seq, n, row0 in ((n_ctx, ctx_len, 0), (n_lat, lat_len, n_ctx_rows)):
            fwd_hi, fwd_lo, inv = dft[n]
            spec = _hyena_filters(n, p["filt_w1"][l], p["filt_b1"][l], p["filt_freq"][l],
                                  p["filt_w2"][l], p["filt_b2"][l], p["filt_w3"][l], fwd_hi, fwd_lo)
            oc.append(_hyena(proj_r, p["conv_c"][l], spec, p["filt_skip"][l], fwd_hi, inv,
                             n_seq=n_seq, seq_len=n, row0=row0))

        o_a = jnp.concatenate([oa_c, oa_l], axis=0)
        o_b = jnp.concatenate([ob_c, ob_l], axis=0)
        o_c = jnp.concatenate(oc, axis=0)
        merged = _merge(o_a, o_b, o_c, p["w_br_a"], p["w_br_b"], p["w_br_c"], proj_r, l)
        x = _matmul_residual(merged, p["w_out"], x, mod4, 2, l, **blk)
        x = _moe(x, p["norm2_g"][l], mod4, p, l, final_g, final_norm=(l == DEPTH - 1), **blk)

        kv = proj_r[:n_ctx_rows, R_KB:R_XC].reshape(n_ctx, ctx_len, 2, H_B, DV_B)
        outs.append((kv[:, :, 0], kv[:, :, 1], sf_c, sb_c))
    return x, outs


def kernel(x_prompt, x_sample, cache_k, cache_v, state_fwd, state_bwd, c, c_ctx, norm1_g, norm2_g, final_g, w_mod, b_mod, w_in, conv_a, a_log, dt_bias, onorm_a, lam, subln_b, conv_c, filt_w1, filt_b1, filt_freq, filt_w2, filt_b2, filt_w3, filt_skip, w_br_a, w_br_b, w_br_c, w_out, w_router, b_router, w_gu, b_gu, w_down, b_down):
    p = dict(norm1_g=norm1_g, norm2_g=norm2_g, w_mod=w_mod, b_mod=b_mod, w_in=w_in, conv_a=conv_a,
             a_log=a_log, dt_bias=dt_bias, onorm_a=onorm_a, lam=lam, subln_b=subln_b, conv_c=conv_c,
             filt_w1=filt_w1, filt_b1=filt_b1, filt_freq=filt_freq, filt_w2=filt_w2,
             filt_b2=filt_b2, filt_w3=filt_w3, filt_skip=filt_skip, w_br_a=w_br_a,
             w_br_b=w_br_b, w_br_c=w_br_c, w_out=w_out, w_router=w_router, b_router=b_router,
             w_gu=w_gu, b_gu=b_gu, w_down=w_down, b_down=b_down)
    n_ctx, ctx_len, d = x_prompt.shape
    n_lat, lat_len, _ = x_sample.shape
    past = cache_k.shape[2]
    x = jnp.concatenate([x_prompt.reshape(n_ctx * ctx_len, d), x_sample.reshape(n_lat * lat_len, d)], axis=0)
    cvec = jnp.concatenate([c_ctx[None, :], c, jnp.zeros((16 - 1 - n_lat, d), F32)], axis=0)
    caches = [(cache_k[:, l].reshape(n_lat, past, W_B), cache_v[:, l].reshape(n_lat, past, W_B),
               state_fwd[:, l], state_bwd[:, l]) for l in range(DEPTH)]
    y, outs = _trunk(x, cvec, p, final_g, caches, n_ctx=n_ctx, ctx_len=ctx_len, n_lat=n_lat, lat_len=lat_len)
    y_prompt = y[:n_ctx * ctx_len].reshape(n_ctx, ctx_len, d)
    y_sample = y[n_ctx * ctx_len:].reshape(n_lat, lat_len, d)
    stack = lambda idx: jnp.stack([o[idx] for o in outs], axis=1)
    return (y_prompt, y_sample, stack(0), stack(1), stack(2), stack(3))
```

```python
import functools
import math

import jax
import jax.numpy as jnp
from jax import lax
from jax.experimental import pallas as pl
from jax.experimental.pallas import tpu as pltpu

F32 = jnp.float32
BF16 = jnp.bfloat16

D_MODEL = 4096
DEPTH = 2
GRID_W = 64
H_A = 12
DK_A = 128
W_A = H_A * DK_A
SHORT_CONV = 3
H_B = 6
DK_B = 128
DV_B = 2 * DK_B
W_B = H_B * DV_B
ROPE_BASE = 10000.0
C_CH = 1024
HYENA_ORDER = 2
FILT_BANDS = 16
FILT_EMB = 1 + 2 * FILT_BANDS
FILT_HIDDEN = 64
HYENA_FAST_DECAY = 0.3
HYENA_SLOW_DECAY = 1.5
HYENA_TARGET = 1e-2
N_EXPERTS = 32
TOP_K = 4
D_FF = 1024
SWIGLU_LIMIT = 7.0
SWIGLU_ALPHA = 1.702
N_MOD = 6
EPS = 1e-6

OFF_AB = 4 * W_A
OFF_REST = OFF_AB + 4 * H_A
R_QB, R_KB, R_VB = 0, W_B, 2 * W_B
R_XC = 3 * W_B
R_GATES = R_XC + 3 * C_CH
N_REST = R_GATES + 3 * D_MODEL

LANES = 128
VMEM_LIMIT = 56 * 1024 * 1024
GDN_CHUNK = 256
GDN_HEADS = 4
ATT_QBLOCK = 256
MOE_TM = 512
MOE_TF = 512
COMBINE_TM = 128


def _cparams(**kw):
    return pltpu.CompilerParams(vmem_limit_bytes=VMEM_LIMIT, **kw)


def _bf(x):
    return x.astype(BF16)


def _dot(a, b):
    return jnp.dot(_bf(a), _bf(b), preferred_element_type=F32)


def _dot_nt(a, b):
    return lax.dot_general(_bf(a), _bf(b), (((1,), (1,)), ((), ())), preferred_element_type=F32)


def _dot_tn(a, b):
    return lax.dot_general(_bf(a), _bf(b), (((0,), (0,)), ((), ())), preferred_element_type=F32)


def _split2(x):
    hi = _bf(x)
    lo = _bf(x - hi.astype(F32))
    return hi, lo


def _split3(x):
    hi = _bf(x)
    r = x - hi.astype(F32)
    mid = _bf(r)
    lo = _bf(r - mid.astype(F32))
    return hi, mid, lo


def _dot_hl(a, b):
    ah, al = _split2(a)
    bh, bl = _split2(b)
    d = functools.partial(jnp.dot, preferred_element_type=F32)
    return d(ah, bh) + (d(ah, bl) + d(al, bh))


def _dot_exact_lhs(m01, x):
    m = _bf(m01)
    h, mi, lo = _split3(x)
    d = functools.partial(jnp.dot, preferred_element_type=F32)
    return d(m, h) + (d(m, mi) + d(m, lo))


def _sigmoid(x):
    return 1.0 / (1.0 + jnp.exp(-x))


def _silu(x):
    return x * _sigmoid(x)


def _softplus(x):
    return jnp.maximum(x, 0.0) + jnp.log(1.0 + jnp.exp(-jnp.abs(x)))


def _mod_row(row_start, n_ctx_rows, rows_per_latent):
    return jnp.where(row_start < n_ctx_rows, 0, 1 + (row_start - n_ctx_rows) // rows_per_latent)


def _mm_kernel(x_ref, w_ref, o_ref, acc_ref, *, nk):
    k = pl.program_id(2)

    @pl.when(k == 0)
    def _():
        acc_ref[...] = jnp.zeros_like(acc_ref)

    acc_ref[...] += _dot(x_ref[...], w_ref[...])

    @pl.when(k == nk - 1)
    def _():
        o_ref[...] = acc_ref[...].astype(o_ref.dtype)


def _mm_bias_kernel(x_ref, w_ref, b_ref, o_ref, acc_ref, *, nk):
    k = pl.program_id(2)

    @pl.when(k == 0)
    def _():
        acc_ref[...] = jnp.zeros_like(acc_ref)

    acc_ref[...] += _dot(x_ref[...], w_ref[...])

    @pl.when(k == nk - 1)
    def _():
        o_ref[...] = (acc_ref[...] + b_ref[...]).astype(o_ref.dtype)


def _layer_spec(block, index_map, layer):
    if layer is None:
        return pl.BlockSpec(block, index_map)
    return pl.BlockSpec((None,) + tuple(block), lambda *a: (layer,) + tuple(index_map(*a)))


def _matmul(x, w, *, n, col0=0, bias=None, tm, tn, tk, out_dtype=F32, name, layer=None):
    m, kdim = x.shape
    assert m % tm == 0 and n % tn == 0 and kdim % tk == 0 and col0 % tn == 0
    nk = kdim // tk
    cb = col0 // tn
    in_specs = [pl.BlockSpec((tm, tk), lambda i, j, k: (i, k)),
                _layer_spec((tk, tn), lambda i, j, k: (k, j + cb), layer)]
    args = [x, w]
    if bias is None:
        body = functools.partial(_mm_kernel, nk=nk)
    else:
        body = functools.partial(_mm_bias_kernel, nk=nk)
        in_specs.append(pl.BlockSpec((1, tn), lambda i, j, k: (0, j)))
        args.append(bias.reshape(1, n))
    return pl.pallas_call(
        body,
        out_shape=jax.ShapeDtypeStruct((m, n), out_dtype),
        grid=(m // tm, n // tn, nk),
        in_specs=in_specs,
        out_specs=pl.BlockSpec((tm, tn), lambda i, j, k: (i, j)),
        scratch_shapes=[pltpu.VMEM((tm, tn), F32)],
        compiler_params=_cparams(),
        name=name,
    )(*args)


def _mm_nt_kernel(x_ref, w_ref, o_ref, w_bf):
    @pl.when(pl.program_id(1) == 0)
    def _():
        w_bf[...] = _bf(w_ref[0])

    o_ref[...] = lax.dot_general(x_ref[...], w_bf[...], (((1,), (1,)), ((), ())),
                                 preferred_element_type=F32).astype(o_ref.dtype)


def _matmul_nt(x, w_t, *, n, row0, layer, tm, tn, name, out_dtype=F32):
    m, kdim = x.shape
    assert m % tm == 0 and n % tn == 0 and row0 % 8 == 0 and tn % 8 == 0
    assert row0 + n <= w_t.shape[1] and x.dtype == BF16
    return pl.pallas_call(
        _mm_nt_kernel,
        out_shape=jax.ShapeDtypeStruct((m, n), out_dtype),
        grid=(n // tn, m // tm),
        in_specs=[pl.BlockSpec((tm, kdim), lambda j, i: (i, 0)),
                  pl.BlockSpec((pl.Element(1), pl.Element(tn), pl.Element(kdim)),
                               lambda j, i: (layer, (row0 // 8 + j * (tn // 8)) * 8, 0))],
        out_specs=pl.BlockSpec((tm, tn), lambda j, i: (i, j)),
        scratch_shapes=[pltpu.VMEM((tn, kdim), BF16)],
        compiler_params=_cparams(),
        name=name,
    )(x, w_t)


def _adaln_kernel(x_ref, g_ref, scale_ref, shift_ref, o_ref):
    x = x_ref[...]
    y = x * lax.rsqrt(jnp.mean(x * x, axis=-1, keepdims=True) + EPS)
    o_ref[...] = (y * g_ref[...] * (1.0 + scale_ref[...]) + shift_ref[...]).astype(o_ref.dtype)


def _adaln(x, g, mod4, shift_idx, scale_idx, *, n_ctx_rows, rows_per_latent, tm=256):
    t, d = x.shape
    row = lambda i: _mod_row(i * tm, n_ctx_rows, rows_per_latent)
    return pl.pallas_call(
        _adaln_kernel,
        out_shape=jax.ShapeDtypeStruct((t, d), BF16),
        grid=(t // tm,),
        in_specs=[pl.BlockSpec((tm, d), lambda i: (i, 0)),
                  pl.BlockSpec((1, d), lambda i: (0, 0)),
                  pl.BlockSpec((None, None, 1, d), lambda i: (row(i), scale_idx, 0, 0)),
                  pl.BlockSpec((None, None, 1, d), lambda i: (row(i), shift_idx, 0, 0))],
        out_specs=pl.BlockSpec((tm, d), lambda i: (i, 0)),
        compiler_params=_cparams(),
        name="adaln",
    )(x, g.reshape(1, d), mod4, mod4)


def _gdn_gate_kernel(ab_ref, alog_ref, dtb_ref, o_ref):
    ab = ab_ref[...]
    tm = ab.shape[0]
    lane = lax.broadcasted_iota(jnp.int32, ab.shape, 1)
    rows = lax.broadcasted_iota(jnp.int32, (tm, tm), 0)
    cols = lax.broadcasted_iota(jnp.int32, (tm, tm), 1)
    g = -jnp.exp(alog_ref[...]) * _softplus(ab + dtb_ref[...])
    prefix = _dot_exact_lhs((cols <= rows).astype(F32), g)
    suffix = _dot_exact_lhs((cols >= rows).astype(F32), g)
    gcum = jnp.where(lane < H_A, prefix, suffix)
    o_ref[...] = jnp.where(lane < 2 * H_A, gcum, _sigmoid(ab))


def _gdn_gates(ab, a_log, dt_bias):
    t = ab.shape[0]
    tm = GDN_CHUNK
    pad = lambda v: jnp.pad(v.reshape(1, 2 * H_A), ((0, 0), (0, LANES - 2 * H_A)))
    return pl.pallas_call(
        _gdn_gate_kernel,
        out_shape=jax.ShapeDtypeStruct((t, LANES), F32),
        grid=(t // tm,),
        in_specs=[pl.BlockSpec((tm, LANES), lambda i: (i, 0)),
                  pl.BlockSpec((1, LANES), lambda i: (0, 0)),
                  pl.BlockSpec((1, LANES), lambda i: (0, 0))],
        out_specs=pl.BlockSpec((tm, LANES), lambda i: (i, 0)),
        compiler_params=_cparams(),
        name="gdn_gates",
    )(ab, pad(a_log), pad(dt_bias))


def _unit_tri_inverse(mats, rows, cols, dot):
    n = mats[0].shape[0]
    eye = (rows == cols).astype(F32)
    same = lambda s: (rows // s) == (cols // s)
    dps = [jnp.where(same(16), a, 0.0) for a in mats]
    ts = [eye - d for d in dps]
    for _ in range(3):
        dbs = [_bf(d) for d in dps]
        dps = [dot(d, d) for d in dbs]
        ts = [t + dot(t, d) for t, d in zip(ts, dps)]
    s = 16
    while s < n:
        mask = same(2 * s) & jnp.logical_not(same(s))
        tbs = [_bf(t) for t in ts]
        lts = [dot(jnp.where(mask, a, 0.0), tb) for a, tb in zip(mats, tbs)]
        ts = [t - dot(tb, lt) for t, tb, lt in zip(ts, tbs, lts)]
        s *= 2
    return ts


def _gdn_kernel(q_ref, k_ref, v_ref, z_ref, gate_ref, cw_ref, og_ref, s0f_ref, s0b_ref,
                o_ref, sf_ref, sb_ref, qs, ks, vs, of_s, ob_s, *, seq_len, chunk, heads):
    h0 = pl.program_id(1) * heads
    n_chunks = seq_len // chunk
    width = heads * DK_A
    head = lambda j: slice(j * DK_A, (j + 1) * DK_A)

    pos = lax.broadcasted_iota(jnp.int32, (seq_len, width), 0)

    def conv_silu(x_ref, w):
        x = x_ref[...]
        prev = jnp.where(pos == 0, 0.0, pltpu.roll(x, 1, 0))
        nxt = jnp.where(pos == seq_len - 1, 0.0, pltpu.roll(x, seq_len - 1, 0))
        return _silu(prev * w[0:1, :] + x * w[1:2, :] + nxt * w[2:3, :])

    def l2n(x):
        return x * lax.rsqrt(jnp.sum(x * x, axis=-1, keepdims=True) + 1e-6)

    qc = conv_silu(q_ref, cw_ref[0])
    kc = conv_silu(k_ref, cw_ref[1])
    for j in range(heads):
        qs[:, head(j)] = l2n(qc[:, head(j)]) * (DK_A ** -0.5)
        ks[:, head(j)] = l2n(kc[:, head(j)])
    vs[...] = conv_silu(v_ref, cw_ref[2])

    rows = lax.broadcasted_iota(jnp.int32, (chunk, chunk), 0)
    cols = lax.broadcasted_iota(jnp.int32, (chunk, chunk), 1)
    lane = lax.broadcasted_iota(jnp.int32, (chunk, LANES), 1)

    chains = [(j, d) for j in range(heads) for d in (0, 1)]
    masks = {0: (cols <= rows, cols < rows),
             1: (cols >= rows, cols > rows)}
    out_refs = {0: of_s, 1: ob_s}

    def body(i, states):
        r0s = {0: pl.multiple_of(i * chunk, chunk), 1: pl.multiple_of((n_chunks - 1 - i) * chunk, chunk)}
        gates = {d: gate_ref[pl.ds(r0s[d], chunk), :] for d in (0, 1)}
        pick = lambda d, idx: jnp.sum(jnp.where(lane == idx, gates[d], 0.0), axis=-1, keepdims=True)
        qkv = [tuple(s[pl.ds(r0s[d], chunk), head(j)] for s in (qs, ks, vs)) for j, d in chains]
        gcum = [jnp.broadcast_to(pick(d, d * H_A + h0 + j), (chunk, LANES)) for j, d in chains]
        beta = [pick(d, (2 + d) * H_A + h0 + j) for j, d in chains]
        last = {0: chunk - 1, 1: 0}
        g_last = [gcm[last[d]:last[d] + 1, :] for (j, d), gcm in zip(chains, gcum)]
        decay = []
        for (j, d), gcm in zip(chains, gcum):
            gc = jnp.concatenate([gcm] * (chunk // LANES), axis=1)
            incl = masks[d][0]
            decay.append(jnp.where(incl, jnp.exp(jnp.where(incl, gc - gc.T, 0.0)), 0.0))
        kk = [_dot_nt(k, k) for q, k, v in qkv]
        a = [jnp.where(masks[d][1], b * x * dc, 0.0) for (j, d), b, x, dc in zip(chains, beta, kk, decay)]
        t = _unit_tri_inverse(a, rows, cols, _dot)
        rhs = [jnp.concatenate([v * b, k * b * jnp.exp(gcm)], axis=-1)
               for (q, k, v), b, gcm in zip(qkv, beta, gcum)]
        uw = [_dot(ti, r) for ti, r in zip(t, rhs)]
        qk = [_dot_nt(q, k) * dc for (q, k, v), dc in zip(qkv, decay)]
        ws = [_dot(jnp.concatenate([x[:, DK_A:], q * jnp.exp(gcm)], axis=0), s)
              for x, (q, k, v), gcm, s in zip(uw, qkv, gcum, states)]
        v_new = [x[:, :DK_A] - y[:chunk] for x, y in zip(uw, ws)]
        o = [y[chunk:] + _dot(m, vn) for y, m, vn in zip(ws, qk, v_new)]
        for (j, d), oi in zip(chains, o):
            out_refs[d][pl.ds(r0s[d], chunk), head(j)] = oi
        return tuple(s * jnp.exp(gl[:, 0:1]) + _dot_tn(k * jnp.exp(gl - gcm), vn)
                     for s, gl, (q, k, v), gcm, vn in zip(states, g_last, qkv, gcum, v_new))

    init = tuple((s0f_ref, s0b_ref)[d][j] for j, d in chains)
    final = lax.fori_loop(0, n_chunks, body, init)
    for (j, d), s in zip(chains, final):
        (sf_ref, sb_ref)[d][j] = s
    for j in range(heads):
        o = of_s[:, head(j)] + ob_s[:, head(j)]
        o = o * lax.rsqrt(jnp.mean(o * o, axis=-1, keepdims=True) + EPS) * og_ref[...]
        o_ref[:, head(j)] = (o * _silu(z_ref[:, head(j)])).astype(o_ref.dtype)


def _gdn(proj_a, gates, conv_w, onorm_g, s0f, s0b, *, n_seq, seq_len, row0, heads=GDN_HEADS):
    assert row0 % seq_len == 0 and seq_len % GDN_CHUNK == 0 and H_A % heads == 0
    rb = row0 // seq_len
    ng = H_A // heads
    width = heads * DK_A
    cw = conv_w.reshape(SHORT_CONV, 3, W_A).transpose(1, 0, 2)
    tok = lambda part: pl.BlockSpec((seq_len, width), lambda b, h: (b + rb, part * ng + h))
    st = pl.BlockSpec((None, heads, DK_A, DK_A), lambda b, h: (b, h, 0, 0))
    body = functools.partial(_gdn_kernel, seq_len=seq_len, chunk=GDN_CHUNK, heads=heads)
    return pl.pallas_call(
        body,
        out_shape=(jax.ShapeDtypeStruct((n_seq * seq_len, W_A), BF16),
                   jax.ShapeDtypeStruct((n_seq, H_A, DK_A, DK_A), F32),
                   jax.ShapeDtypeStruct((n_seq, H_A, DK_A, DK_A), F32)),
        grid=(n_seq, ng),
        in_specs=[tok(0), tok(1), tok(2), tok(3),
                  pl.BlockSpec((seq_len, LANES), lambda b, h: (b + rb, 0)),
                  pl.BlockSpec((3, SHORT_CONV, width), lambda b, h: (0, 0, h)),
                  pl.BlockSpec((1, DK_A), lambda b, h: (0, 0)),
                  st, st],
        out_specs=(pl.BlockSpec((seq_len, width), lambda b, h: (b, h)), st, st),
        scratch_shapes=[pltpu.VMEM((seq_len, width), F32) for _ in range(5)],
        compiler_params=_cparams(),
        name=f"gdn_L{seq_len}",
    )(proj_a, proj_a, proj_a, proj_a, gates, cw, onorm_g.reshape(1, DK_A), s0f, s0b)


def _rope(x, cos, sin_signed, lane):
    rot = jnp.where((lane % 64) < 32, pltpu.roll(x, LANES - 32, 1), pltpu.roll(x, 32, 1))
    return x * cos + rot * sin_signed


def _attn_kernel(*refs, seq_len, qblock, use_rope, n_cache, lam_init):
    it = iter(refs)
    q_ref, k_ref, v_ref, lam_ref, g_ref = next(it), next(it), next(it), next(it), next(it)
    cos_ref = sin_ref = ck_ref = cv_ref = None
    if use_rope:
        cos_ref, sin_ref = next(it), next(it)
    if n_cache:
        ck_ref, cv_ref = next(it), next(it)
    o_ref = next(it)
    ks = next(it)

    lam = lam_ref[...]
    lam_full = (jnp.exp(jnp.sum(lam[0:1] * lam[1:2], axis=-1, keepdims=True))
                - jnp.exp(jnp.sum(lam[2:3] * lam[3:4], axis=-1, keepdims=True)) + lam_init)
    scale = DK_B ** -0.5
    lane = lax.broadcasted_iota(jnp.int32, (seq_len, DK_B), 1) if use_rope else None
    lane_q = lax.broadcasted_iota(jnp.int32, (qblock, DK_B), 1) if use_rope else None
    for r in range(2):
        kr = k_ref[:, r * DK_B:(r + 1) * DK_B]
        if use_rope:
            kr = _rope(kr, cos_ref[...], sin_ref[...], lane)
        ks[r] = _bf(kr)
    v = _bf(v_ref[...])
    for qb in range(seq_len // qblock):
        sl = slice(qb * qblock, (qb + 1) * qblock)
        probs = []
        for r in range(2):
            qr = q_ref[sl, r * DK_B:(r + 1) * DK_B]
            if use_rope:
                qr = _rope(qr, cos_ref[sl, :], sin_ref[sl, :], lane_q)
            s = _dot_nt(qr, ks[r]) * scale
            m = jnp.max(s, axis=-1, keepdims=True)
            if n_cache:
                sc = _dot_nt(qr, ck_ref[:, r * DK_B:(r + 1) * DK_B]) * scale
                m = jnp.maximum(m, jnp.max(sc, axis=-1, keepdims=True))
                ec = jnp.exp(sc - m)
            e = jnp.exp(s - m)
            den = jnp.sum(e, axis=-1, keepdims=True)
            if n_cache:
                den = den + jnp.sum(ec, axis=-1, keepdims=True)
                probs.append((e / den, ec / den))
            else:
                probs.append((e / den, None))
        o = _dot(probs[0][0] - lam_full * probs[1][0], v)
        if n_cache:
            o = o + _dot(probs[0][1] - lam_full * probs[1][1], cv_ref[...])
        o = o * lax.rsqrt(jnp.mean(o * o, axis=-1, keepdims=True) + 1e-5) * g_ref[...]
        o_ref[sl, :] = (o * (1.0 - lam_init)).astype(o_ref.dtype)


def _attention(proj_r, lam, subln_g, lam_init, *, n_seq, seq_len, row0, rope=None, cache=None):
    assert row0 % seq_len == 0
    rb = row0 // seq_len
    nh = H_B
    blk = lambda part: pl.BlockSpec((seq_len, DV_B), lambda b, h: (b + rb, part * nh + h))
    in_specs = [blk(0), blk(1), blk(2),
                pl.BlockSpec((4, DK_B), lambda b, h: (0, 0)),
                pl.BlockSpec((1, DV_B), lambda b, h: (0, 0))]
    args = [proj_r, proj_r, proj_r, lam, subln_g.reshape(1, DV_B)]
    if rope is not None:
        in_specs += [pl.BlockSpec((seq_len, DK_B), lambda b, h: (0, 0))] * 2
        args += list(rope)
    n_cache = 0
    if cache is not None:
        ck, cv = cache
        n_cache = ck.shape[1]
        in_specs += [pl.BlockSpec((None, n_cache, DV_B), lambda b, h: (b, 0, h))] * 2
        args += [ck, cv]
    body = functools.partial(_attn_kernel, seq_len=seq_len, qblock=min(ATT_QBLOCK, seq_len),
                             use_rope=rope is not None, n_cache=n_cache, lam_init=lam_init)
    return pl.pallas_call(
        body,
        out_shape=jax.ShapeDtypeStruct((n_seq * seq_len, W_B), BF16),
        grid=(n_seq, nh),
        in_specs=in_specs,
        out_specs=pl.BlockSpec((seq_len, DV_B), lambda b, h: (b, h)),
        scratch_shapes=[pltpu.VMEM((2, seq_len, DK_B), BF16)],
        compiler_params=_cparams(),
        name=f"diff_attn_L{seq_len}",
    )(*args)


def _rope_tables(n_tok):
    rows = n_tok // GRID_W
    row = jnp.repeat(jnp.arange(rows), GRID_W)
    col = jnp.tile(jnp.arange(GRID_W), rows)
    half = DK_B // 2
    inv = ROPE_BASE ** (-jnp.arange(0, half, 2, dtype=F32) / half)
    ang = jnp.stack([row, col], axis=-1).astype(F32)[..., None] * inv
    cos, sin = jnp.cos(ang), jnp.sin(ang)
    cos_t = jnp.concatenate([cos, cos], axis=-1).reshape(n_tok, DK_B)
    sin_t = jnp.concatenate([-sin, sin], axis=-1).reshape(n_tok, DK_B)
    return cos_t, sin_t


def _dft_tables(n):
    f = jnp.arange(n, dtype=jnp.int32)[:, None]
    t = jnp.arange(n, dtype=jnp.int32)[None, :]
    ang = ((f * t) % (2 * n)).astype(F32) * (math.pi / n)
    cos, sin = jnp.cos(ang), jnp.sin(ang)
    nyq = jnp.where(t % 2 == 0, 1.0, -1.0).astype(F32)
    fwd_im = jnp.where(f == 0, nyq, -sin)
    fwd = jnp.concatenate([cos, fwd_im], axis=0)
    wgt = jnp.where(f == 0, 1.0, 2.0).astype(F32) / (2 * n)
    inv_re = (wgt * cos).T
    inv_im = jnp.where(f == 0, nyq / (2 * n), -wgt * sin).T
    inv = jnp.concatenate([inv_re, inv_im], axis=1)
    return fwd, inv


def _spec_mul(u, s, n, row):
    ur, ui, sr, si = u[:n], u[n:], s[:n], s[n:]
    first = row == 0
    yr = ur * sr - jnp.where(first, 0.0, ui * si)
    yi = jnp.where(first, ui * si, ur * si + ui * sr)
    return yr, yi


def _hyena_filter_kernel(z_ref, w1_ref, b1_ref, fr_ref, w2_ref, b2_ref, w3_ref, win_ref,
                         fh_ref, fl_ref, o_ref, *, n):
    fr = fr_ref[...]
    h = jnp.sin(fr * (_dot_hl(z_ref[...], w1_ref[...]) + b1_ref[...]))
    h = jnp.sin(fr * (_dot_hl(h, w2_ref[...]) + b2_ref[...]))
    win = win_ref[...]
    row = lax.broadcasted_iota(jnp.int32, win.shape, 0)
    fwd_hi, fwd_lo = fh_ref[...], fl_ref[...]

    def dft(x):
        xh, xl = _split2(x)
        d = functools.partial(jnp.dot, preferred_element_type=F32)
        return d(fwd_hi, xh) + (d(fwd_hi, xl) + d(fwd_lo, xh))

    for o in range(HYENA_ORDER):
        hf = _dot_hl(h, w3_ref[2 * o]) * win
        hb = jnp.where(row == 0, 0.0, _dot_hl(h, w3_ref[2 * o + 1]) * win)
        p, q = dft(hf), dft(hb)
        o_ref[o, :n, :] = p[:n] + q[:n]
        o_ref[o, n:, :] = jnp.where(row == 0, p[n:] + q[n:], p[n:] - q[n:])


def _hyena_filters(n, w1, b1, freq, w2, b2, w3, fwd_hi, fwd_lo, tc=256):
    t = jnp.linspace(0.0, 1.0, n, dtype=F32)[:, None]
    wpos = 2.0 * math.pi * jnp.arange(n, dtype=F32)[:, None] / n
    f = jnp.linspace(1e-4, FILT_BANDS - 1, FILT_BANDS, dtype=F32)
    z = jnp.concatenate([t, jnp.cos(wpos * f), -jnp.sin(wpos * f)], axis=-1)
    z = jnp.pad(z, ((0, 0), (0, LANES - FILT_EMB)))
    w1p = jnp.pad(w1, ((0, LANES - FILT_EMB), (0, 0)))
    max_decay = math.log(HYENA_TARGET) / HYENA_FAST_DECAY
    min_decay = math.log(HYENA_TARGET) / HYENA_SLOW_DECAY
    deltas = jnp.linspace(min_decay, max_decay, C_CH, dtype=F32)
    window = jnp.exp(-t * jnp.abs(deltas))
    w3r = w3.reshape(FILT_HIDDEN, 2 * HYENA_ORDER, C_CH).transpose(1, 0, 2)
    full = lambda shape: pl.BlockSpec(shape, lambda j: (0,) * len(shape))
    return pl.pallas_call(
        functools.partial(_hyena_filter_kernel, n=n),
        out_shape=jax.ShapeDtypeStruct((HYENA_ORDER, 2 * n, C_CH), F32),
        grid=(C_CH // tc,),
        in_specs=[full((n, LANES)), full((LANES, FILT_HIDDEN)), full((1, FILT_HIDDEN)),
                  full((1, FILT_HIDDEN)), full((FILT_HIDDEN, FILT_HIDDEN)), full((1, FILT_HIDDEN)),
                  pl.BlockSpec((2 * HYENA_ORDER, FILT_HIDDEN, tc), lambda j: (0, 0, j)),
                  pl.BlockSpec((n, tc), lambda j: (0, j)),
                  full((2 * n, n)), full((2 * n, n))],
        out_specs=pl.BlockSpec((HYENA_ORDER, 2 * n, tc), lambda j: (0, 0, j)),
        compiler_params=_cparams(),
        name=f"hyena_filter_L{n}",
    )(z, w1p, b1.reshape(1, -1), freq.reshape(1, -1), w2, b2.reshape(1, -1), w3r, window,
      fwd_hi, fwd_lo)


def _hyena_kernel(x1_ref, x2_ref, v_ref, cw_ref, spec_ref, skip_ref, fwd_ref, inv_ref, o_ref, *, n):
    shape = v_ref.shape
    row = lax.broadcasted_iota(jnp.int32, shape, 0)

    def conv3(x_ref, p):
        x = x_ref[...]
        prev = jnp.where(row == 0, 0.0, pltpu.roll(x, 1, 0))
        nxt = jnp.where(row == n - 1, 0.0, pltpu.roll(x, n - 1, 0))
        return prev * cw_ref[p, 0:1, :] + x * cw_ref[p, 1:2, :] + nxt * cw_ref[p, 2:3, :]

    z = conv3(v_ref, 2)
    fwd, inv = fwd_ref[...], inv_ref[...]
    for o, gate_ref in enumerate((x1_ref, x2_ref)):
        u = jnp.dot(fwd, _bf(z), preferred_element_type=F32)
        yr, yi = _spec_mul(u, spec_ref[o], n, row)
        y = jnp.dot(inv, _bf(jnp.concatenate([yr, yi], axis=0)), preferred_element_type=F32)
        z = conv3(gate_ref, o) * (y + z * skip_ref[o:o + 1, :])
    o_ref[...] = z.astype(o_ref.dtype)


def _hyena(proj_r, conv_w, spec, skip, fwd, inv, *, n_seq, seq_len, row0, tc=256):
    assert row0 % seq_len == 0
    rb = row0 // seq_len
    xb = lambda part: pl.BlockSpec((seq_len, tc), lambda b, j: (b + rb, (R_XC + part * C_CH) // tc + j))
    cw = conv_w.reshape(SHORT_CONV, 3, C_CH).transpose(1, 0, 2)
    return pl.pallas_call(
        functools.partial(_hyena_kernel, n=seq_len),
        out_shape=jax.ShapeDtypeStruct((n_seq * seq_len, C_CH), BF16),
        grid=(n_seq, C_CH // tc),
        in_specs=[xb(0), xb(1), xb(2),
                  pl.BlockSpec((3, SHORT_CONV, tc), lambda b, j: (0, 0, j)),
                  pl.BlockSpec((HYENA_ORDER, 2 * seq_len, tc), lambda b, j: (0, 0, j)),
                  pl.BlockSpec((HYENA_ORDER, tc), lambda b, j: (0, j)),
                  pl.BlockSpec((2 * seq_len, seq_len), lambda b, j: (0, 0)),
                  pl.BlockSpec((seq_len, 2 * seq_len), lambda b, j: (0, 0))],
        out_specs=pl.BlockSpec((seq_len, tc), lambda b, j: (b, j)),
        compiler_params=_cparams(),
        name=f"hyena_L{seq_len}",
    )(proj_r, proj_r, proj_r, cw, spec, skip, fwd, inv)


def _merge_kernel(oa_ref, ob_ref, oc_ref, wa_ref, wb_ref, wc_ref, ga_ref, gb_ref, gc_ref, o_ref,
                  wa_bf, wb_bf, wc_bf):
    @pl.when(pl.program_id(1) == 0)
    def _():
        wa_bf[...] = _bf(wa_ref[...])
        wb_bf[...] = _bf(wb_ref[...])
        wc_bf[...] = _bf(wc_ref[...])

    d = functools.partial(jnp.dot, preferred_element_type=F32)
    acc = _sigmoid(ga_ref[...]) * d(oa_ref[...], wa_bf[...])
    acc += _sigmoid(gb_ref[...]) * d(ob_ref[...], wb_bf[...])
    acc += _sigmoid(gc_ref[...]) * d(oc_ref[...], wc_bf[...])
    o_ref[...] = acc.astype(o_ref.dtype)


def _merge(o_a, o_b, o_c, w_a, w_b, w_c, proj_r, layer, tm=512, tn=512):
    t = o_a.shape[0]
    d = D_MODEL
    gate = lambda part: pl.BlockSpec((tm, tn), lambda j, i: (i, (R_GATES + part * d) // tn + j))
    act = lambda width: pl.BlockSpec((tm, width), lambda j, i: (i, 0))
    wgt = lambda width: _layer_spec((width, tn), lambda j, i: (0, j), layer)
    return pl.pallas_call(
        _merge_kernel,
        out_shape=jax.ShapeDtypeStruct((t, d), BF16),
        grid=(d // tn, t // tm),
        in_specs=[act(W_A), act(W_B), act(C_CH), wgt(W_A), wgt(W_B), wgt(C_CH),
                  gate(0), gate(1), gate(2)],
        out_specs=pl.BlockSpec((tm, tn), lambda j, i: (i, j)),
        scratch_shapes=[pltpu.VMEM((W_A, tn), BF16), pltpu.VMEM((W_B, tn), BF16), pltpu.VMEM((C_CH, tn), BF16)],
        compiler_params=_cparams(),
        name="merge",
    )(o_a, o_b, o_c, w_a, w_b, w_c, proj_r, proj_r, proj_r)


def _mm_resid_kernel(y_ref, w_ref, x_ref, gate_ref, o_ref, w_bf):
    @pl.when(pl.program_id(1) == 0)
    def _():
        w_bf[...] = _bf(w_ref[...])

    o_ref[...] = x_ref[...] + gate_ref[...] * jnp.dot(y_ref[...], w_bf[...], preferred_element_type=F32)


def _matmul_residual(y, w, x, mod4, gate_idx, layer, *, n_ctx_rows, rows_per_latent, tm=512, tn=512):
    t, kdim = y.shape
    n = w.shape[-1]
    assert y.dtype == BF16
    row = lambda i: _mod_row(i * tm, n_ctx_rows, rows_per_latent)
    return pl.pallas_call(
        _mm_resid_kernel,
        out_shape=jax.ShapeDtypeStruct((t, n), F32),
        grid=(n // tn, t // tm),
        in_specs=[pl.BlockSpec((tm, kdim), lambda j, i: (i, 0)),
                  _layer_spec((kdim, tn), lambda j, i: (0, j), layer),
                  pl.BlockSpec((tm, tn), lambda j, i: (i, j)),
                  pl.BlockSpec((None, None, 1, tn), lambda j, i: (row(i), gate_idx, 0, j))],
        out_specs=pl.BlockSpec((tm, tn), lambda j, i: (i, j)),
        scratch_shapes=[pltpu.VMEM((kdim, tn), BF16)],
        compiler_params=_cparams(),
        name="out_proj_residual",
    )(y, w, x, mod4)


def _router_kernel(x_ref, g_ref, scale_ref, shift_ref, wr_ref, br_ref, h_ref, ti_ref, tw_ref):
    x = x_ref[...]
    y = x * lax.rsqrt(jnp.mean(x * x, axis=-1, keepdims=True) + EPS)
    h = y * g_ref[...] * (1.0 + scale_ref[...]) + shift_ref[...]
    half = h.shape[1] // 2
    h_ref[...] = _pack_bf16_pair(h[:, :half], h[:, half:])
    logits = _dot_hl(h, wr_ref[...]) + br_ref[...]
    lane_i = lax.broadcasted_iota(jnp.int32, logits.shape, 1)
    lane = lane_i.astype(F32)
    neg = jnp.float32(-jnp.inf)
    cur = jnp.where(lane_i < N_EXPERTS, logits, neg)
    vals = []
    ti = jnp.zeros(logits.shape, F32)
    for kk in range(TOP_K):
        m = jnp.max(cur, axis=-1, keepdims=True)
        idx = jnp.min(jnp.where(cur == m, lane, float(LANES)), axis=-1, keepdims=True)
        ti = jnp.where(lane_i == kk, idx, ti)
        vals.append(m)
        cur = jnp.where(lane == idx, neg, cur)
    es = [jnp.exp(vk - vals[0]) for vk in vals]
    den = es[0] + es[1] + es[2] + es[3]
    tw = jnp.zeros(logits.shape, F32)
    for kk in range(TOP_K):
        tw = jnp.where(lane_i == kk, es[kk] / den, tw)
    ti_ref[...] = ti.astype(jnp.int32)
    tw_ref[...] = tw


def _router(x, g, mod4, shift_idx, scale_idx, w_router, b_router, *, n_ctx_rows, rows_per_latent, tm=256):
    t, d = x.shape
    row = lambda i: _mod_row(i * tm, n_ctx_rows, rows_per_latent)
    wr = jnp.pad(w_router, ((0, 0), (0, LANES - N_EXPERTS)))
    br = jnp.pad(b_router.reshape(1, -1), ((0, 0), (0, LANES - N_EXPERTS)))
    return pl.pallas_call(
        _router_kernel,
        out_shape=(jax.ShapeDtypeStruct((t, d // 2), jnp.uint32),
                   jax.ShapeDtypeStruct((t, LANES), jnp.int32),
                   jax.ShapeDtypeStruct((t, LANES), F32)),
        grid=(t // tm,),
        in_specs=[pl.BlockSpec((tm, d), lambda i: (i, 0)),
                  pl.BlockSpec((1, d), lambda i: (0, 0)),
                  pl.BlockSpec((None, None, 1, d), lambda i: (row(i), scale_idx, 0, 0)),
                  pl.BlockSpec((None, None, 1, d), lambda i: (row(i), shift_idx, 0, 0)),
                  pl.BlockSpec((d, LANES), lambda i: (0, 0)),
                  pl.BlockSpec((1, LANES), lambda i: (0, 0))],
        out_specs=(pl.BlockSpec((tm, d // 2), lambda i: (i, 0)),
                   pl.BlockSpec((tm, LANES), lambda i: (i, 0)),
                   pl.BlockSpec((tm, LANES), lambda i: (i, 0))),
        compiler_params=_cparams(),
        name="router",
    )(x, g.reshape(1, d), mod4, mod4, wr, br)


def _row_copy(src_hbm, dst_vmem, sem, src_row, dst_row):
    return pltpu.make_async_copy(src_hbm.at[pl.ds(src_row, 1), :], dst_vmem.at[pl.ds(dst_row, 1), :], sem)


def _gather_kernel(ta_ref, src_ref, nxt_ref, h_hbm, o_ref, buf, sem, *, tm):
    i = pl.program_id(0)
    last = pl.num_programs(0) - 1
    slot = i % 2

    unroll = 8
    stride = tm // unroll

    def fetch(idx_ref, s):
        def start(a, _):
            for b in range(unroll):
                r = b * stride + a
                _row_copy(h_hbm, buf.at[s], sem.at[s], idx_ref[0, r], r).start()
            return 0

        lax.fori_loop(0, stride, start, 0)

    def drain(s):
        def wait(a, _):
            for b in range(unroll):
                r = b * stride + a
                _row_copy(h_hbm, buf.at[s], sem.at[s], src_ref[0, r], r).wait()
            return 0

        lax.fori_loop(0, stride, wait, 0)
        lo, hi = _unpack_bf16_pair(buf[s])
        half = lo.shape[1]
        o_ref[:, :half] = lo.astype(o_ref.dtype)
        o_ref[:, half:] = hi.astype(o_ref.dtype)

    @pl.when(jnp.logical_and(i == 0, ta_ref[0] == 1))
    def _():
        fetch(src_ref, 0)

    next_active = jnp.logical_and(i < last, ta_ref[jnp.minimum(i + 1, last)] == 1)
    for s in (0, 1):
        @pl.when(jnp.logical_and(next_active, slot == 1 - s))
        def _():
            fetch(nxt_ref, s)

    for s in (0, 1):
        @pl.when(jnp.logical_and(ta_ref[i] == 1, slot == s))
        def _():
            drain(s)

    @pl.when(ta_ref[i] == 0)
    def _():
        o_ref[...] = jnp.zeros_like(o_ref)


def _moe_gather(h_packed, src_token, tile_active, n_rows):
    h = h_packed
    d = 2 * h.shape[1]
    tm = MOE_TM
    n_tiles = n_rows // tm
    src3 = src_token.reshape(n_tiles, 1, tm)
    return pl.pallas_call(
        functools.partial(_gather_kernel, tm=tm),
        out_shape=jax.ShapeDtypeStruct((n_rows, d), BF16),
        grid_spec=pltpu.PrefetchScalarGridSpec(
            num_scalar_prefetch=1,
            grid=(n_tiles,),
            in_specs=[pl.BlockSpec((None, 1, tm), lambda i, ta: (i, 0, 0), memory_space=pltpu.SMEM),
                      pl.BlockSpec((None, 1, tm), lambda i, ta: (jnp.minimum(i + 1, n_tiles - 1), 0, 0),
                                   memory_space=pltpu.SMEM),
                      pl.BlockSpec(memory_space=pl.ANY)],
            out_specs=pl.BlockSpec((tm, d), lambda i, ta: (i, 0)),
            scratch_shapes=[pltpu.VMEM((2, tm, d // 2), jnp.uint32), pltpu.SemaphoreType.DMA((2,))]),
        compiler_params=_cparams(dimension_semantics=("arbitrary",)),
        name="moe_gather",
    )(tile_active, src3, src3, h)


def _moe_up_kernel(te_ref, ta_ref, ts_ref, x_ref, wg_ref, wu_ref, bg_ref, bu_ref, rw_ref, o_ref):
    i = pl.program_id(1)

    @pl.when(ta_ref[i] == 1)
    def _():
        x = x_ref[...]
        gate = _dot(x, wg_ref[...]) + bg_ref[...]
        up = _dot(x, wu_ref[...]) + bu_ref[...]
        gate = jnp.minimum(gate, SWIGLU_LIMIT)
        up = jnp.clip(up, -SWIGLU_LIMIT, SWIGLU_LIMIT)
        act = (up + 1.0) * gate * _sigmoid(SWIGLU_ALPHA * gate)
        o_ref[...] = (act * rw_ref[...]).astype(o_ref.dtype)

    @pl.when(ta_ref[i] == 0)
    def _():
        o_ref[...] = jnp.zeros_like(o_ref)


def _moe_up(x_sorted, w_gu, b_gu, row_w, tile_expert, tile_active, tile_src, layer):
    p, d = x_sorted.shape
    tm, tf = MOE_TM, MOE_TF
    nf = D_FF // tf
    b3 = b_gu.reshape(N_EXPERTS, 1, 2 * D_FF)
    return pl.pallas_call(
        _moe_up_kernel,
        out_shape=jax.ShapeDtypeStruct((p, D_FF), BF16),
        grid_spec=pltpu.PrefetchScalarGridSpec(
            num_scalar_prefetch=3,
            grid=(nf, p // tm),
            in_specs=[pl.BlockSpec((tm, d), lambda j, i, te, ta, ts: (ts[i], 0)),
                      pl.BlockSpec((None, None, d, tf), lambda j, i, te, ta, ts: (layer, te[i], 0, j)),
                      pl.BlockSpec((None, None, d, tf), lambda j, i, te, ta, ts: (layer, te[i], 0, nf + j)),
                      pl.BlockSpec((None, 1, tf), lambda j, i, te, ta, ts: (te[i], 0, j)),
                      pl.BlockSpec((None, 1, tf), lambda j, i, te, ta, ts: (te[i], 0, nf + j)),
                      pl.BlockSpec((tm, 1), lambda j, i, te, ta, ts: (ts[i], 0))],
            out_specs=pl.BlockSpec((tm, tf), lambda j, i, te, ta, ts: (i, j)),
            scratch_shapes=[]),
        compiler_params=_cparams(),
        name="moe_up",
    )(tile_expert, tile_active, tile_src, x_sorted, w_gu, w_gu, b3, b3, row_w)


def _pack_bf16_pair(lo, hi):
    lo_bits = lax.bitcast_convert_type(_bf(lo).astype(F32), jnp.uint32)
    hi_bits = lax.bitcast_convert_type(_bf(hi).astype(F32), jnp.uint32)
    return hi_bits | (lo_bits >> 16)


def _unpack_bf16_pair(u):
    lo = lax.bitcast_convert_type(u << 16, F32)
    hi = lax.bitcast_convert_type(u & jnp.uint32(0xFFFF0000), F32)
    return lo, hi


def _moe_down_kernel(te_ref, ta_ref, ts_ref, a_ref, wlo_ref, whi_ref, blo_ref, bhi_ref, rw_ref, o_ref,
                     wlo_s, whi_s):
    i = pl.program_id(1)
    changed = jnp.logical_or(i == 0, te_ref[i] != te_ref[jnp.maximum(i - 1, 0)])

    @pl.when(changed)
    def _():
        wlo_s[...] = _bf(wlo_ref[...])
        whi_s[...] = _bf(whi_ref[...])

    @pl.when(ta_ref[i] == 1)
    def _():
        a = a_ref[...]
        rw = rw_ref[...]
        y_lo = jnp.dot(a, wlo_s[...], preferred_element_type=F32) + rw * blo_ref[...]
        y_hi = jnp.dot(a, whi_s[...], preferred_element_type=F32) + rw * bhi_ref[...]
        o_ref[...] = _pack_bf16_pair(y_lo, y_hi)

    @pl.when(ta_ref[i] == 0)
    def _():
        o_ref[...] = jnp.zeros_like(o_ref)


def _moe_down(act, w_down, b_down, row_w, tile_expert, tile_active, tile_src, layer, tn=1024):
    p, f = act.shape
    d = w_down.shape[-1]
    half = d // 2
    nh = half // tn
    tm = MOE_TM
    b3 = b_down.reshape(N_EXPERTS, 1, d)
    wspec = lambda off: pl.BlockSpec((None, None, f, tn), lambda j, i, te, ta, ts: (layer, te[i], 0, off + j))
    bspec = lambda off: pl.BlockSpec((None, 1, tn), lambda j, i, te, ta, ts: (te[i], 0, off + j))
    return pl.pallas_call(
        _moe_down_kernel,
        out_shape=jax.ShapeDtypeStruct((p, half), jnp.uint32),
        grid_spec=pltpu.PrefetchScalarGridSpec(
            num_scalar_prefetch=3,
            grid=(nh, p // tm),
            in_specs=[pl.BlockSpec((tm, f), lambda j, i, te, ta, ts: (ts[i], 0)),
                      wspec(0), wspec(nh), bspec(0), bspec(nh),
                      pl.BlockSpec((tm, 1), lambda j, i, te, ta, ts: (ts[i], 0))],
            out_specs=pl.BlockSpec((tm, tn), lambda j, i, te, ta, ts: (i, j)),
            scratch_shapes=[pltpu.VMEM((f, tn), BF16), pltpu.VMEM((f, tn), BF16)]),
        compiler_params=_cparams(),
        name="moe_down",
    )(tile_expert, tile_active, tile_src, act, w_down, w_down, b3, b3, row_w)


def _combine_kernel(dest_ref, nxt_ref, y_hbm, x_ref, gate_ref, fg_ref, o_ref, buf, sem, *, tm, final_norm):
    i = pl.program_id(0)
    last = pl.num_programs(0) - 1
    slot = i % 2

    def fetch(idx_ref, s):
        def start(r, _):
            for kk in range(TOP_K):
                _row_copy(y_hbm, buf.at[s, kk], sem.at[s], idx_ref[0, r * TOP_K + kk], r).start()
            return 0

        lax.fori_loop(0, tm, start, 0)

    def finish(s):
        def wait(r, _):
            for kk in range(TOP_K):
                _row_copy(y_hbm, buf.at[s, kk], sem.at[s], dest_ref[0, r * TOP_K + kk], r).wait()
            return 0

        lax.fori_loop(0, tm, wait, 0)
        parts = [_unpack_bf16_pair(buf[s, kk]) for kk in range(TOP_K)]
        half = buf.shape[-1]
        xs = []
        for side, cols in enumerate((slice(0, half), slice(half, 2 * half))):
            y = (parts[0][side] + parts[1][side]) + (parts[2][side] + parts[3][side])
            xs.append(x_ref[:, cols] + gate_ref[:, cols] * y)
        if final_norm:
            ms = (jnp.sum(xs[0] * xs[0], axis=-1, keepdims=True)
                  + jnp.sum(xs[1] * xs[1], axis=-1, keepdims=True)) / (2 * half)
            inv = lax.rsqrt(ms + EPS)
            xs = [xs[0] * inv * fg_ref[:, :half], xs[1] * inv * fg_ref[:, half:]]
        o_ref[:, :half] = xs[0]
        o_ref[:, half:] = xs[1]

    @pl.when(i == 0)
    def _():
        fetch(dest_ref, 0)

    for s in (0, 1):
        @pl.when(jnp.logical_and(i < last, slot == 1 - s))
        def _():
            fetch(nxt_ref, s)

    for s in (0, 1):
        @pl.when(slot == s)
        def _():
            finish(s)


def _moe_combine(y_sorted, dest, x, mod4, gate_idx, final_g, *, final_norm, n_ctx_rows, rows_per_latent):
    t, d = x.shape
    tm = COMBINE_TM
    row = lambda i: _mod_row(i * tm, n_ctx_rows, rows_per_latent)
    n_tiles = t // tm
    dest3 = dest.reshape(n_tiles, 1, tm * TOP_K)
    return pl.pallas_call(
        functools.partial(_combine_kernel, tm=tm, final_norm=final_norm),
        out_shape=jax.ShapeDtypeStruct((t, d), F32),
        grid=(n_tiles,),
        in_specs=[pl.BlockSpec((None, 1, tm * TOP_K), lambda i: (i, 0, 0), memory_space=pltpu.SMEM),
                  pl.BlockSpec((None, 1, tm * TOP_K), lambda i: (jnp.minimum(i + 1, n_tiles - 1), 0, 0),
                               memory_space=pltpu.SMEM),
                  pl.BlockSpec(memory_space=pl.ANY),
                  pl.BlockSpec((tm, d), lambda i: (i, 0)),
                  pl.BlockSpec((None, None, 1, d), lambda i: (row(i), gate_idx, 0, 0)),
                  pl.BlockSpec((1, d), lambda i: (0, 0))],
        out_specs=pl.BlockSpec((tm, d), lambda i: (i, 0)),
        scratch_shapes=[pltpu.VMEM((2, TOP_K, tm, d // 2), jnp.uint32), pltpu.SemaphoreType.DMA((2,))],
        compiler_params=_cparams(dimension_semantics=("arbitrary",)),
        name="moe_combine",
    )(dest3, dest3, y_sorted, x, mod4, final_g.reshape(1, d))


def _moe_plan(top_i, n_tiles):
    t = top_i.shape[0]
    tm = MOE_TM
    e_flat = top_i.reshape(-1)
    onehot = (e_flat[:, None] == jnp.arange(N_EXPERTS, dtype=jnp.int32)[None, :]).astype(jnp.int32)
    csum = jnp.cumsum(onehot, axis=0)
    counts = csum[-1]
    rank = jnp.sum(onehot * csum, axis=1) - 1
    tiles_per = (counts + tm - 1) // tm
    tile_end = jnp.cumsum(tiles_per)
    group_row0 = (tile_end - tiles_per) * tm
    dest = group_row0[e_flat] + rank
    n_used = tile_end[-1]
    tile_ids = jnp.arange(n_tiles, dtype=jnp.int32)
    tile_expert = jnp.searchsorted(tile_end, tile_ids, side="right").astype(jnp.int32)
    tile_active = (tile_ids < n_used).astype(jnp.int32)
    last_expert = jnp.searchsorted(tile_end, n_used - 1, side="right").astype(jnp.int32)
    tile_expert = jnp.where(tile_active == 1, tile_expert, last_expert)
    tile_src = jnp.minimum(tile_ids, n_used - 1)
    return dest.astype(jnp.int32), tile_expert, tile_active, tile_src


def _moe(x, norm_g, mod4, p, l, final_g, *, final_norm, n_ctx_rows, rows_per_latent):
    t = x.shape[0]
    blk = dict(n_ctx_rows=n_ctx_rows, rows_per_latent=rows_per_latent)
    h2, top_i, top_w = _router(x, norm_g, mod4, 3, 4, p["w_router"][l], p["b_router"][l], **blk)
    top_i, top_w = top_i[:, :TOP_K], top_w[:, :TOP_K]
    n_rows = t * TOP_K + N_EXPERTS * MOE_TM
    dest, tile_expert, tile_active, tile_src = _moe_plan(top_i, n_rows // MOE_TM)
    slot_of_row = jnp.full((n_rows,), -1, jnp.int32).at[dest].set(jnp.arange(t * TOP_K, dtype=jnp.int32))
    real = slot_of_row >= 0
    src_token = jnp.where(real, slot_of_row // TOP_K, jnp.arange(n_rows, dtype=jnp.int32) % t)
    row_w = jnp.where(real, top_w.reshape(-1)[jnp.maximum(slot_of_row, 0)], 0.0).reshape(n_rows, 1)
    x_sorted = _moe_gather(h2, src_token, tile_active, n_rows)
    act = _moe_up(x_sorted, p["w_gu"], p["b_gu"][l], row_w, tile_expert, tile_active, tile_src, l)
    y_sorted = _moe_down(act, p["w_down"], p["b_down"][l], row_w, tile_expert, tile_active, tile_src, l)
    return _moe_combine(y_sorted, dest, x, mod4, 5, final_g, final_norm=final_norm, **blk)


def _trunk(x, cvec, p, final_g, caches, *, n_ctx, ctx_len, n_lat, lat_len):
    n_ctx_rows = n_ctx * ctx_len
    blk = dict(n_ctx_rows=n_ctx_rows, rows_per_latent=lat_len)
    rope = _rope_tables(lat_len)
    dft = {}
    for n in (ctx_len, lat_len):
        fwd, inv = _dft_tables(n)
        fwd_hi = _bf(fwd)
        dft[n] = (fwd_hi, _bf(fwd - fwd_hi.astype(F32)), _bf(inv))
    silu_c = jax.nn.silu(cvec)
    w_in_t = jnp.swapaxes(p["w_in"], 1, 2)
    outs = []
    for l in range(DEPTH):
        mod = _matmul(silu_c, p["w_mod"], n=N_MOD * D_MODEL, bias=p["b_mod"][l],
                      tm=16, tn=2048, tk=1024, name="modulation", layer=l)
        mod4 = mod.reshape(16, N_MOD, 1, D_MODEL)
        h = _adaln(x, p["norm1_g"][l], mod4, 0, 1, **blk)
        in_proj = functools.partial(_matmul_nt, h, w_in_t, layer=l)
        proj_a = in_proj(n=OFF_AB, row0=0, tm=512, tn=768, name="in_proj_a")
        ab = in_proj(n=LANES, row0=OFF_AB, tm=1024, tn=LANES, name="in_proj_ab")
        proj_r = in_proj(n=N_REST, row0=OFF_REST, tm=512, tn=768, name="in_proj_rest")

        ck, cv, s0f, s0b = caches[l]
        gates = _gdn_gates(ab, p["a_log"][l], p["dt_bias"][l])
        zeros_state = jnp.zeros((n_ctx, H_A, DK_A, DK_A), F32)
        gdn = functools.partial(_gdn, proj_a, gates, p["conv_a"][l], p["onorm_a"][l])
        oa_c, sf_c, sb_c = gdn(zeros_state, zeros_state, n_seq=n_ctx, seq_len=ctx_len, row0=0)
        oa_l, _, _ = gdn(s0f, s0b, n_seq=n_lat, seq_len=lat_len, row0=n_ctx_rows)

        lam_init = 0.8 - 0.6 * math.exp(-0.3 * l)
        attn = functools.partial(_attention, proj_r, p["lam"][l], p["subln_b"][l], lam_init)
        ob_c = attn(n_seq=n_ctx, seq_len=ctx_len, row0=0)
        ob_l = attn(n_seq=n_lat, seq_len=lat_len, row0=n_ctx_rows, rope=rope, cache=(ck, cv))

        oc = []
        for n_seq, n, row0 in ((n_ctx, ctx_len, 0), (n_lat, lat_len, n_ctx_rows)):
            fwd_hi, fwd_lo, inv = dft[n]
            spec = _hyena_filters(n, p["filt_w1"][l], p["filt_b1"][l], p["filt_freq"][l],
                                  p["filt_w2"][l], p["filt_b2"][l], p["filt_w3"][l], fwd_hi, fwd_lo)
            oc.append(_hyena(proj_r, p["conv_c"][l], spec, p["filt_skip"][l], fwd_hi, inv,
                             n_seq=n_seq, seq_len=n, row0=row0))

        o_a = jnp.concatenate([oa_c, oa_l], axis=0)
        o_b = jnp.concatenate([ob_c, ob_l], axis=0)
        o_c = jnp.concatenate(oc, axis=0)
        merged = _merge(o_a, o_b, o_c, p["w_br_a"], p["w_br_b"], p["w_br_c"], proj_r, l)
        x = _matmul_residual(merged, p["w_out"], x, mod4, 2, l, **blk)
        x = _moe(x, p["norm2_g"][l], mod4, p, l, final_g, final_norm=(l == DEPTH - 1), **blk)

        kv = proj_r[:n_ctx_rows, R_KB:R_XC].reshape(n_ctx, ctx_len, 2, H_B, DV_B)
        outs.append((kv[:, :, 0], kv[:, :, 1], sf_c, sb_c))
    return x, outs


def kernel(x_prompt, x_sample, cache_k, cache_v, state_fwd, state_bwd, c, c_ctx, norm1_g, norm2_g, final_g, w_mod, b_mod, w_in, conv_a, a_log, dt_bias, onorm_a, lam, subln_b, conv_c, filt_w1, filt_b1, filt_freq, filt_w2, filt_b2, filt_w3, filt_skip, w_br_a, w_br_b, w_br_c, w_out, w_router, b_router, w_gu, b_gu, w_down, b_down):
    p = dict(norm1_g=norm1_g, norm2_g=norm2_g, w_mod=w_mod, b_mod=b_mod, w_in=w_in, conv_a=conv_a,
             a_log=a_log, dt_bias=dt_bias, onorm_a=onorm_a, lam=lam, subln_b=subln_b, conv_c=conv_c,
             filt_w1=filt_w1, filt_b1=filt_b1, filt_freq=filt_freq, filt_w2=filt_w2,
             filt_b2=filt_b2, filt_w3=filt_w3, filt_skip=filt_skip, w_br_a=w_br_a,
             w_br_b=w_br_b, w_br_c=w_br_c, w_out=w_out, w_router=w_router, b_router=b_router,
             w_gu=w_gu, b_gu=b_gu, w_down=w_down, b_down=b_down)
    n_ctx, ctx_len, d = x_prompt.shape
    n_lat, lat_len, _ = x_sample.shape
    past = cache_k.shape[2]
    x = jnp.concatenate([x_prompt.reshape(n_ctx * ctx_len, d), x_sample.reshape(n_lat * lat_len, d)], axis=0)
    cvec = jnp.concatenate([c_ctx[None, :], c, jnp.zeros((16 - 1 - n_lat, d), F32)], axis=0)
    caches = [(cache_k[:, l].reshape(n_lat, past, W_B), cache_v[:, l].reshape(n_lat, past, W_B),
               state_fwd[:, l], state_bwd[:, l]) for l in range(DEPTH)]
    y, outs = _trunk(x, cvec, p, final_g, caches, n_ctx=n_ctx, ctx_len=ctx_len, n_lat=n_lat, lat_len=lat_len)
    y_prompt = y[:n_ctx * ctx_len].reshape(n_ctx, ctx_len, d)
    y_sample = y[n_ctx * ctx_len:].reshape(n_lat, lat_len, d)
    stack = lambda idx: jnp.stack([o[idx] for o in outs], axis=1)
    return (y_prompt, y_sample, stack(0), stack(1), stack(2), stack(3))
```

```python
import functools
import math

import jax
import jax.numpy as jnp
from jax import lax
from jax.experimental import pallas as pl
from jax.experimental.pallas import tpu as pltpu

F32 = jnp.float32
BF16 = jnp.bfloat16

D_MODEL = 4096
DEPTH = 2
GRID_W = 64
H_A = 12
DK_A = 128
W_A = H_A * DK_A
SHORT_CONV = 3
H_B = 6
DK_B = 128
DV_B = 2 * DK_B
W_B = H_B * DV_B
ROPE_BASE = 10000.0
C_CH = 1024
HYENA_ORDER = 2
FILT_BANDS = 16
FILT_EMB = 1 + 2 * FILT_BANDS
FILT_HIDDEN = 64
HYENA_FAST_DECAY = 0.3
HYENA_SLOW_DECAY = 1.5
HYENA_TARGET = 1e-2
N_EXPERTS = 32
TOP_K = 4
D_FF = 1024
SWIGLU_LIMIT = 7.0
SWIGLU_ALPHA = 1.702
N_MOD = 6
EPS = 1e-6

OFF_AB = 4 * W_A
OFF_REST = OFF_AB + 4 * H_A
R_QB, R_KB, R_VB = 0, W_B, 2 * W_B
R_XC = 3 * W_B
R_GATES = R_XC + 3 * C_CH
N_REST = R_GATES + 3 * D_MODEL

LANES = 128
VMEM_LIMIT = 56 * 1024 * 1024
GDN_CHUNK = 256
GDN_HEADS = 4
ATT_QBLOCK = 256
MOE_TM = 512
MOE_TF = 512
COMBINE_TM = 128


def _cparams(**kw):
    return pltpu.CompilerParams(vmem_limit_bytes=VMEM_LIMIT, **kw)


def _bf(x):
    return x.astype(BF16)


def _dot(a, b):
    return jnp.dot(_bf(a), _bf(b), preferred_element_type=F32)


def _dot_nt(a, b):
    return lax.dot_general(_bf(a), _bf(b), (((1,), (1,)), ((), ())), preferred_element_type=F32)


def _dot_tn(a, b):
    return lax.dot_general(_bf(a), _bf(b), (((0,), (0,)), ((), ())), preferred_element_type=F32)


def _split2(x):
    hi = _bf(x)
    lo = _bf(x - hi.astype(F32))
    return hi, lo


def _split3(x):
    hi = _bf(x)
    r = x - hi.astype(F32)
    mid = _bf(r)
    lo = _bf(r - mid.astype(F32))
    return hi, mid, lo


def _dot_hl(a, b):
    ah, al = _split2(a)
    bh, bl = _split2(b)
    d = functools.partial(jnp.dot, preferred_element_type=F32)
    return d(ah, bh) + (d(ah, bl) + d(al, bh))


def _dot_exact_lhs(m01, x):
    m = _bf(m01)
    h, mi, lo = _split3(x)
    d = functools.partial(jnp.dot, preferred_element_type=F32)
    return d(m, h) + (d(m, mi) + d(m, lo))


def _sigmoid(x):
    return 0.5 * jnp.tanh(0.5 * x) + 0.5


def _silu(x):
    return x * _sigmoid(x)


def _softplus(x):
    return jnp.maximum(x, 0.0) + jnp.log(1.0 + jnp.exp(-jnp.abs(x)))


def _mod_row(row_start, n_ctx_rows, rows_per_latent):
    return jnp.where(row_start < n_ctx_rows, 0, 1 + (row_start - n_ctx_rows) // rows_per_latent)


def _mm_kernel(x_ref, w_ref, o_ref, acc_ref, *, nk):
    k = pl.program_id(2)

    @pl.when(k == 0)
    def _():
        acc_ref[...] = jnp.zeros_like(acc_ref)

    acc_ref[...] += _dot(x_ref[...], w_ref[...])

    @pl.when(k == nk - 1)
    def _():
        o_ref[...] = acc_ref[...].astype(o_ref.dtype)


def _mm_bias_kernel(x_ref, w_ref, b_ref, o_ref, acc_ref, *, nk):
    k = pl.program_id(2)

    @pl.when(k == 0)
    def _():
        acc_ref[...] = jnp.zeros_like(acc_ref)

    acc_ref[...] += _dot(x_ref[...], w_ref[...])

    @pl.when(k == nk - 1)
    def _():
        o_ref[...] = (acc_ref[...] + b_ref[...]).astype(o_ref.dtype)


def _layer_spec(block, index_map, layer):
    if layer is None:
        return pl.BlockSpec(block, index_map)
    return pl.BlockSpec((None,) + tuple(block), lambda *a: (layer,) + tuple(index_map(*a)))


def _matmul(x, w, *, n, col0=0, bias=None, tm, tn, tk, out_dtype=F32, name, layer=None):
    m, kdim = x.shape
    assert m % tm == 0 and n % tn == 0 and kdim % tk == 0 and col0 % tn == 0
    nk = kdim // tk
    cb = col0 // tn
    in_specs = [pl.BlockSpec((tm, tk), lambda i, j, k: (i, k)),
                _layer_spec((tk, tn), lambda i, j, k: (k, j + cb), layer)]
    args = [x, w]
    if bias is None:
        body = functools.partial(_mm_kernel, nk=nk)
    else:
        body = functools.partial(_mm_bias_kernel, nk=nk)
        in_specs.append(pl.BlockSpec((1, tn), lambda i, j, k: (0, j)))
        args.append(bias.reshape(1, n))
    return pl.pallas_call(
        body,
        out_shape=jax.ShapeDtypeStruct((m, n), out_dtype),
        grid=(m // tm, n // tn, nk),
        in_specs=in_specs,
        out_specs=pl.BlockSpec((tm, tn), lambda i, j, k: (i, j)),
        scratch_shapes=[pltpu.VMEM((tm, tn), F32)],
        compiler_params=_cparams(),
        name=name,
    )(*args)


def _mm_nt_kernel(x_ref, w_ref, o_ref, w_bf):
    @pl.when(pl.program_id(1) == 0)
    def _():
        w_bf[...] = _bf(w_ref[0])

    o_ref[...] = lax.dot_general(x_ref[...], w_bf[...], (((1,), (1,)), ((), ())),
                                 preferred_element_type=F32).astype(o_ref.dtype)


def _matmul_nt(x, w_t, *, n, row0, layer, tm, tn, name, out_dtype=F32):
    m, kdim = x.shape
    assert m % tm == 0 and n % tn == 0 and row0 % 8 == 0 and tn % 8 == 0
    assert row0 + n <= w_t.shape[1] and x.dtype == BF16
    return pl.pallas_call(
        _mm_nt_kernel,
        out_shape=jax.ShapeDtypeStruct((m, n), out_dtype),
        grid=(n // tn, m // tm),
        in_specs=[pl.BlockSpec((tm, kdim), lambda j, i: (i, 0)),
                  pl.BlockSpec((pl.Element(1), pl.Element(tn), pl.Element(kdim)),
                               lambda j, i: (layer, (row0 // 8 + j * (tn // 8)) * 8, 0))],
        out_specs=pl.BlockSpec((tm, tn), lambda j, i: (i, j)),
        scratch_shapes=[pltpu.VMEM((tn, kdim), BF16)],
        compiler_params=_cparams(),
        name=name,
    )(x, w_t)


def _adaln_kernel(x_ref, g_ref, scale_ref, shift_ref, o_ref):
    x = x_ref[...]
    y = x * lax.rsqrt(jnp.mean(x * x, axis=-1, keepdims=True) + EPS)
    o_ref[...] = (y * g_ref[...] * (1.0 + scale_ref[...]) + shift_ref[...]).astype(o_ref.dtype)


def _adaln(x, g, mod4, shift_idx, scale_idx, *, n_ctx_rows, rows_per_latent, tm=256):
    t, d = x.shape
    row = lambda i: _mod_row(i * tm, n_ctx_rows, rows_per_latent)
    return pl.pallas_call(
        _adaln_kernel,
        out_shape=jax.ShapeDtypeStruct((t, d), BF16),
        grid=(t // tm,),
        in_specs=[pl.BlockSpec((tm, d), lambda i: (i, 0)),
                  pl.BlockSpec((1, d), lambda i: (0, 0)),
                  pl.BlockSpec((None, None, 1, d), lambda i: (row(i), scale_idx, 0, 0)),
                  pl.BlockSpec((None, None, 1, d), lambda i: (row(i), shift_idx, 0, 0))],
        out_specs=pl.BlockSpec((tm, d), lambda i: (i, 0)),
        compiler_params=_cparams(),
        name="adaln",
    )(x, g.reshape(1, d), mod4, mod4)


def _gdn_gate_kernel(ab_ref, alog_ref, dtb_ref, o_ref):
    ab = ab_ref[...]
    tm = ab.shape[0]
    lane = lax.broadcasted_iota(jnp.int32, ab.shape, 1)
    rows = lax.broadcasted_iota(jnp.int32, (tm, tm), 0)
    cols = lax.broadcasted_iota(jnp.int32, (tm, tm), 1)
    g = -jnp.exp(alog_ref[...]) * _softplus(ab + dtb_ref[...])
    prefix = _dot_exact_lhs((cols <= rows).astype(F32), g)
    suffix = _dot_exact_lhs((cols >= rows).astype(F32), g)
    gcum = jnp.where(lane < H_A, prefix, suffix)
    o_ref[...] = jnp.where(lane < 2 * H_A, gcum, _sigmoid(ab))


def _gdn_gates(ab, a_log, dt_bias):
    t = ab.shape[0]
    tm = GDN_CHUNK
    pad = lambda v: jnp.pad(v.reshape(1, 2 * H_A), ((0, 0), (0, LANES - 2 * H_A)))
    return pl.pallas_call(
        _gdn_gate_kernel,
        out_shape=jax.ShapeDtypeStruct((t, LANES), F32),
        grid=(t // tm,),
        in_specs=[pl.BlockSpec((tm, LANES), lambda i: (i, 0)),
                  pl.BlockSpec((1, LANES), lambda i: (0, 0)),
                  pl.BlockSpec((1, LANES), lambda i: (0, 0))],
        out_specs=pl.BlockSpec((tm, LANES), lambda i: (i, 0)),
        compiler_params=_cparams(),
        name="gdn_gates",
    )(ab, pad(a_log), pad(dt_bias))


def _unit_tri_inverse(mats, rows, cols, dot):
    n = mats[0].shape[0]
    eye = (rows == cols).astype(F32)
    same = lambda s: (rows // s) == (cols // s)
    dps = [jnp.where(same(16), a, 0.0) for a in mats]
    ts = [eye - d for d in dps]
    for _ in range(3):
        dbs = [_bf(d) for d in dps]
        dps = [dot(d, d) for d in dbs]
        ts = [t + dot(t, d) for t, d in zip(ts, dps)]
    s = 16
    while s < n:
        mask = same(2 * s) & jnp.logical_not(same(s))
        tbs = [_bf(t) for t in ts]
        lts = [dot(jnp.where(mask, a, 0.0), tb) for a, tb in zip(mats, tbs)]
        ts = [t - dot(tb, lt) for t, tb, lt in zip(ts, tbs, lts)]
        s *= 2
    return ts


def _gdn_kernel(q_ref, k_ref, v_ref, z_ref, gate_ref, cw_ref, og_ref, s0f_ref, s0b_ref,
                o_ref, sf_ref, sb_ref, qs, ks, vs, of_s, ob_s, *, seq_len, chunk, heads):
    h0 = pl.program_id(1) * heads
    n_chunks = seq_len // chunk
    width = heads * DK_A
    head = lambda j: slice(j * DK_A, (j + 1) * DK_A)

    pos = lax.broadcasted_iota(jnp.int32, (seq_len, width), 0)

    def conv_silu(x_ref, w):
        x = x_ref[...]
        prev = jnp.where(pos == 0, 0.0, pltpu.roll(x, 1, 0))
        nxt = jnp.where(pos == seq_len - 1, 0.0, pltpu.roll(x, seq_len - 1, 0))
        return _silu(prev * w[0:1, :] + x * w[1:2, :] + nxt * w[2:3, :])

    def l2n(x):
        return x * lax.rsqrt(jnp.sum(x * x, axis=-1, keepdims=True) + 1e-6)

    qc = conv_silu(q_ref, cw_ref[0])
    kc = conv_silu(k_ref, cw_ref[1])
    for j in range(heads):
        qs[:, head(j)] = l2n(qc[:, head(j)]) * (DK_A ** -0.5)
        ks[:, head(j)] = l2n(kc[:, head(j)])
    vs[...] = conv_silu(v_ref, cw_ref[2])

    rows = lax.broadcasted_iota(jnp.int32, (chunk, chunk), 0)
    cols = lax.broadcasted_iota(jnp.int32, (chunk, chunk), 1)
    lane = lax.broadcasted_iota(jnp.int32, (chunk, LANES), 1)

    chains = [(j, d) for j in range(heads) for d in (0, 1)]
    masks = {0: (cols <= rows, cols < rows),
             1: (cols >= rows, cols > rows)}
    out_refs = {0: of_s, 1: ob_s}

    def body(i, states):
        r0s = {0: pl.multiple_of(i * chunk, chunk), 1: pl.multiple_of((n_chunks - 1 - i) * chunk, chunk)}
        gates = {d: gate_ref[pl.ds(r0s[d], chunk), :] for d in (0, 1)}
        pick = lambda d, idx: jnp.sum(jnp.where(lane == idx, gates[d], 0.0), axis=-1, keepdims=True)
        qkv = [tuple(s[pl.ds(r0s[d], chunk), head(j)] for s in (qs, ks, vs)) for j, d in chains]
        gcum = [jnp.broadcast_to(pick(d, d * H_A + h0 + j), (chunk, LANES)) for j, d in chains]
        beta = [pick(d, (2 + d) * H_A + h0 + j) for j, d in chains]
        last = {0: chunk - 1, 1: 0}
        g_last = [gcm[last[d]:last[d] + 1, :] for (j, d), gcm in zip(chains, gcum)]
        decay = []
        for (j, d), gcm in zip(chains, gcum):
            gc = jnp.concatenate([gcm] * (chunk // LANES), axis=1)
            incl = masks[d][0]
            decay.append(jnp.where(incl, jnp.exp(jnp.where(incl, gc - gc.T, 0.0)), 0.0))
        kk = [_dot_nt(k, k) for q, k, v in qkv]
        a = [jnp.where(masks[d][1], b * x * dc, 0.0) for (j, d), b, x, dc in zip(chains, beta, kk, decay)]
        t = _unit_tri_inverse(a, rows, cols, _dot)
        rhs = [jnp.concatenate([v * b, k * b * jnp.exp(gcm)], axis=-1)
               for (q, k, v), b, gcm in zip(qkv, beta, gcum)]
        uw = [_dot(ti, r) for ti, r in zip(t, rhs)]
        qk = [_dot_nt(q, k) * dc for (q, k, v), dc in zip(qkv, decay)]
        ws = [_dot(jnp.concatenate([x[:, DK_A:], q * jnp.exp(gcm)], axis=0), s)
              for x, (q, k, v), gcm, s in zip(uw, qkv, gcum, states)]
        v_new = [x[:, :DK_A] - y[:chunk] for x, y in zip(uw, ws)]
        o = [y[chunk:] + _dot(m, vn) for y, m, vn in zip(ws, qk, v_new)]
        for (j, d), oi in zip(chains, o):
            out_refs[d][pl.ds(r0s[d], chunk), head(j)] = oi
        return tuple(s * jnp.exp(gl[:, 0:1]) + _dot_tn(k * jnp.exp(gl - gcm), vn)
                     for s, gl, (q, k, v), gcm, vn in zip(states, g_last, qkv, gcum, v_new))

    init = tuple((s0f_ref, s0b_ref)[d][j] for j, d in chains)
    final = lax.fori_loop(0, n_chunks, body, init)
    for (j, d), s in zip(chains, final):
        (sf_ref, sb_ref)[d][j] = s
    for j in range(heads):
        o = of_s[:, head(j)] + ob_s[:, head(j)]
        o = o * lax.rsqrt(jnp.mean(o * o, axis=-1, keepdims=True) + EPS) * og_ref[...]
        o_ref[:, head(j)] = (o * _silu(z_ref[:, head(j)])).astype(o_ref.dtype)


def _gdn(proj_a, gates, conv_w, onorm_g, s0f, s0b, *, n_seq, seq_len, row0, heads=GDN_HEADS):
    assert row0 % seq_len == 0 and seq_len % GDN_CHUNK == 0 and H_A % heads == 0
    rb = row0 // seq_len
    ng = H_A // heads
    width = heads * DK_A
    cw = conv_w.reshape(SHORT_CONV, 3, W_A).transpose(1, 0, 2)
    tok = lambda part: pl.BlockSpec((seq_len, width), lambda b, h: (b + rb, part * ng + h))
    st = pl.BlockSpec((None, heads, DK_A, DK_A), lambda b, h: (b, h, 0, 0))
    body = functools.partial(_gdn_kernel, seq_len=seq_len, chunk=GDN_CHUNK, heads=heads)
    return pl.pallas_call(
        body,
        out_shape=(jax.ShapeDtypeStruct((n_seq * seq_len, W_A), BF16),
                   jax.ShapeDtypeStruct((n_seq, H_A, DK_A, DK_A), F32),
                   jax.ShapeDtypeStruct((n_seq, H_A, DK_A, DK_A), F32)),
        grid=(n_seq, ng),
        in_specs=[tok(0), tok(1), tok(2), tok(3),
                  pl.BlockSpec((seq_len, LANES), lambda b, h: (b + rb, 0)),
                  pl.BlockSpec((3, SHORT_CONV, width), lambda b, h: (0, 0, h)),
                  pl.BlockSpec((1, DK_A), lambda b, h: (0, 0)),
                  st, st],
        out_specs=(pl.BlockSpec((seq_len, width), lambda b, h: (b, h)), st, st),
        scratch_shapes=[pltpu.VMEM((seq_len, width), F32) for _ in range(5)],
        compiler_params=_cparams(),
        name=f"gdn_L{seq_len}",
    )(proj_a, proj_a, proj_a, proj_a, gates, cw, onorm_g.reshape(1, DK_A), s0f, s0b)


def _rope(x, cos, sin_signed, lane):
    rot = jnp.where((lane % 64) < 32, pltpu.roll(x, LANES - 32, 1), pltpu.roll(x, 32, 1))
    return x * cos + rot * sin_signed


def _attn_kernel(*refs, seq_len, qblock, use_rope, n_cache, lam_init):
    it = iter(refs)
    q_ref, k_ref, v_ref, lam_ref, g_ref = next(it), next(it), next(it), next(it), next(it)
    cos_ref = sin_ref = ck_ref = cv_ref = None
    if use_rope:
        cos_ref, sin_ref = next(it), next(it)
    if n_cache:
        ck_ref, cv_ref = next(it), next(it)
    o_ref = next(it)
    ks = next(it)

    lam = lam_ref[...]
    lam_full = (jnp.exp(jnp.sum(lam[0:1] * lam[1:2], axis=-1, keepdims=True))
                - jnp.exp(jnp.sum(lam[2:3] * lam[3:4], axis=-1, keepdims=True)) + lam_init)
    scale = DK_B ** -0.5
    lane = lax.broadcasted_iota(jnp.int32, (seq_len, DK_B), 1) if use_rope else None
    lane_q = lax.broadcasted_iota(jnp.int32, (qblock, DK_B), 1) if use_rope else None
    for r in range(2):
        kr = k_ref[:, r * DK_B:(r + 1) * DK_B]
        if use_rope:
            kr = _rope(kr, cos_ref[...], sin_ref[...], lane)
        ks[r] = _bf(kr)
    v = _bf(v_ref[...])
    for qb in range(seq_len // qblock):
        sl = slice(qb * qblock, (qb + 1) * qblock)
        probs = []
        for r in range(2):
            qr = q_ref[sl, r * DK_B:(r + 1) * DK_B]
            if use_rope:
                qr = _rope(qr, cos_ref[sl, :], sin_ref[sl, :], lane_q)
            s = _dot_nt(qr, ks[r]) * scale
            m = jnp.max(s, axis=-1, keepdims=True)
            if n_cache:
                sc = _dot_nt(qr, ck_ref[:, r * DK_B:(r + 1) * DK_B]) * scale
                m = jnp.maximum(m, jnp.max(sc, axis=-1, keepdims=True))
                ec = jnp.exp(sc - m)
            e = jnp.exp(s - m)
            den = jnp.sum(e, axis=-1, keepdims=True)
            if n_cache:
                den = den + jnp.sum(ec, axis=-1, keepdims=True)
                probs.append((e / den, ec / den))
            else:
                probs.append((e / den, None))
        o = _dot(probs[0][0] - lam_full * probs[1][0], v)
        if n_cache:
            o = o + _dot(probs[0][1] - lam_full * probs[1][1], cv_ref[...])
        o = o * lax.rsqrt(jnp.mean(o * o, axis=-1, keepdims=True) + 1e-5) * g_ref[...]
        o_ref[sl, :] = (o * (1.0 - lam_init)).astype(o_ref.dtype)


def _attention(proj_r, lam, subln_g, lam_init, *, n_seq, seq_len, row0, rope=None, cache=None):
    assert row0 % seq_len == 0
    rb = row0 // seq_len
    nh = H_B
    blk = lambda part: pl.BlockSpec((seq_len, DV_B), lambda b, h: (b + rb, part * nh + h))
    in_specs = [blk(0), blk(1), blk(2),
                pl.BlockSpec((4, DK_B), lambda b, h: (0, 0)),
                pl.BlockSpec((1, DV_B), lambda b, h: (0, 0))]
    args = [proj_r, proj_r, proj_r, lam, subln_g.reshape(1, DV_B)]
    if rope is not None:
        in_specs += [pl.BlockSpec((seq_len, DK_B), lambda b, h: (0, 0))] * 2
        args += list(rope)
    n_cache = 0
    if cache is not None:
        ck, cv = cache
        n_cache = ck.shape[1]
        in_specs += [pl.BlockSpec((None, n_cache, DV_B), lambda b, h: (b, 0, h))] * 2
        args += [ck, cv]
    body = functools.partial(_attn_kernel, seq_len=seq_len, qblock=min(ATT_QBLOCK, seq_len),
                             use_rope=rope is not None, n_cache=n_cache, lam_init=lam_init)
    return pl.pallas_call(
        body,
        out_shape=jax.ShapeDtypeStruct((n_seq * seq_len, W_B), BF16),
        grid=(n_seq, nh),
        in_specs=in_specs,
        out_specs=pl.BlockSpec((seq_len, DV_B), lambda b, h: (b, h)),
        scratch_shapes=[pltpu.VMEM((2, seq_len, DK_B), BF16)],
        compiler_params=_cparams(),
        name=f"diff_attn_L{seq_len}",
    )(*args)


def _rope_tables(n_tok):
    rows = n_tok // GRID_W
    row = jnp.repeat(jnp.arange(rows), GRID_W)
    col = jnp.tile(jnp.arange(GRID_W), rows)
    half = DK_B // 2
    inv = ROPE_BASE ** (-jnp.arange(0, half, 2, dtype=F32) / half)
    ang = jnp.stack([row, col], axis=-1).astype(F32)[..., None] * inv
    cos, sin = jnp.cos(ang), jnp.sin(ang)
    cos_t = jnp.concatenate([cos, cos], axis=-1).reshape(n_tok, DK_B)
    sin_t = jnp.concatenate([-sin, sin], axis=-1).reshape(n_tok, DK_B)
    return cos_t, sin_t


def _dft_tables(n):
    f = jnp.arange(n, dtype=jnp.int32)[:, None]
    t = jnp.arange(n, dtype=jnp.int32)[None, :]
    ang = ((f * t) % (2 * n)).astype(F32) * (math.pi / n)
    cos, sin = jnp.cos(ang), jnp.sin(ang)
    nyq = jnp.where(t % 2 == 0, 1.0, -1.0).astype(F32)
    fwd_im = jnp.where(f == 0, nyq, -sin)
    fwd = jnp.concatenate([cos, fwd_im], axis=0)
    wgt = jnp.where(f == 0, 1.0, 2.0).astype(F32) / (2 * n)
    inv_re = (wgt * cos).T
    inv_im = jnp.where(f == 0, nyq / (2 * n), -wgt * sin).T
    inv = jnp.concatenate([inv_re, inv_im], axis=1)
    return fwd, inv


def _spec_mul(u, s, n, row):
    ur, ui, sr, si = u[:n], u[n:], s[:n], s[n:]
    first = row == 0
    yr = ur * sr - jnp.where(first, 0.0, ui * si)
    yi = jnp.where(first, ui * si, ur * si + ui * sr)
    return yr, yi


def _hyena_filter_kernel(z_ref, w1_ref, b1_ref, fr_ref, w2_ref, b2_ref, w3_ref, win_ref,
                         fh_ref, fl_ref, o_ref, *, n):
    fr = fr_ref[...]
    h = jnp.sin(fr * (_dot_hl(z_ref[...], w1_ref[...]) + b1_ref[...]))
    h = jnp.sin(fr * (_dot_hl(h, w2_ref[...]) + b2_ref[...]))
    win = win_ref[...]
    row = lax.broadcasted_iota(jnp.int32, win.shape, 0)
    fwd_hi, fwd_lo = fh_ref[...], fl_ref[...]

    def dft(x):
        xh, xl = _split2(x)
        d = functools.partial(jnp.dot, preferred_element_type=F32)
        return d(fwd_hi, xh) + (d(fwd_hi, xl) + d(fwd_lo, xh))

    for o in range(HYENA_ORDER):
        hf = _dot_hl(h, w3_ref[2 * o]) * win
        hb = jnp.where(row == 0, 0.0, _dot_hl(h, w3_ref[2 * o + 1]) * win)
        p, q = dft(hf), dft(hb)
        o_ref[o, :n, :] = p[:n] + q[:n]
        o_ref[o, n:, :] = jnp.where(row == 0, p[n:] + q[n:], p[n:] - q[n:])


def _hyena_filters(n, w1, b1, freq, w2, b2, w3, fwd_hi, fwd_lo, tc=256):
    t = jnp.linspace(0.0, 1.0, n, dtype=F32)[:, None]
    wpos = 2.0 * math.pi * jnp.arange(n, dtype=F32)[:, None] / n
    f = jnp.linspace(1e-4, FILT_BANDS - 1, FILT_BANDS, dtype=F32)
    z = jnp.concatenate([t, jnp.cos(wpos * f), -jnp.sin(wpos * f)], axis=-1)
    z = jnp.pad(z, ((0, 0), (0, LANES - FILT_EMB)))
    w1p = jnp.pad(w1, ((0, LANES - FILT_EMB), (0, 0)))
    max_decay = math.log(HYENA_TARGET) / HYENA_FAST_DECAY
    min_decay = math.log(HYENA_TARGET) / HYENA_SLOW_DECAY
    deltas = jnp.linspace(min_decay, max_decay, C_CH, dtype=F32)
    window = jnp.exp(-t * jnp.abs(deltas))
    w3r = w3.reshape(FILT_HIDDEN, 2 * HYENA_ORDER, C_CH).transpose(1, 0, 2)
    full = lambda shape: pl.BlockSpec(shape, lambda j: (0,) * len(shape))
    return pl.pallas_call(
        functools.partial(_hyena_filter_kernel, n=n),
        out_shape=jax.ShapeDtypeStruct((HYENA_ORDER, 2 * n, C_CH), F32),
        grid=(C_CH // tc,),
        in_specs=[full((n, LANES)), full((LANES, FILT_HIDDEN)), full((1, FILT_HIDDEN)),
                  full((1, FILT_HIDDEN)), full((FILT_HIDDEN, FILT_HIDDEN)), full((1, FILT_HIDDEN)),
                  pl.BlockSpec((2 * HYENA_ORDER, FILT_HIDDEN, tc), lambda j: (0, 0, j)),
                  pl.BlockSpec((n, tc), lambda j: (0, j)),
                  full((2 * n, n)), full((2 * n, n))],
        out_specs=pl.BlockSpec((HYENA_ORDER, 2 * n, tc), lambda j: (0, 0, j)),
        compiler_params=_cparams(),
        name=f"hyena_filter_L{n}",
    )(z, w1p, b1.reshape(1, -1), freq.reshape(1, -1), w2, b2.reshape(1, -1), w3r, window,
      fwd_hi, fwd_lo)


def _hyena_kernel(x1_ref, x2_ref, v_ref, cw_ref, spec_ref, skip_ref, fwd_ref, inv_ref, o_ref, *, n):
    shape = v_ref.shape
    row = lax.broadcasted_iota(jnp.int32, shape, 0)

    def conv3(x_ref, p):
        x = x_ref[...]
        prev = jnp.where(row == 0, 0.0, pltpu.roll(x, 1, 0))
        nxt = jnp.where(row == n - 1, 0.0, pltpu.roll(x, n - 1, 0))
        return prev * cw_ref[p, 0:1, :] + x * cw_ref[p, 1:2, :] + nxt * cw_ref[p, 2:3, :]

    z = conv3(v_ref, 2)
    fwd, inv = fwd_ref[...], inv_ref[...]
    for o, gate_ref in enumerate((x1_ref, x2_ref)):
        u = jnp.dot(fwd, _bf(z), preferred_element_type=F32)
        yr, yi = _spec_mul(u, spec_ref[o], n, row)
        y = jnp.dot(inv, _bf(jnp.concatenate([yr, yi], axis=0)), preferred_element_type=F32)
        z = conv3(gate_ref, o) * (y + z * skip_ref[o:o + 1, :])
    o_ref[...] = z.astype(o_ref.dtype)


def _hyena(proj_r, conv_w, spec, skip, fwd, inv, *, n_seq, seq_len, row0, tc=256):
    assert row0 % seq_len == 0
    rb = row0 // seq_len
    xb = lambda part: pl.BlockSpec((seq_len, tc), lambda b, j: (b + rb, (R_XC + part * C_CH) // tc + j))
    cw = conv_w.reshape(SHORT_CONV, 3, C_CH).transpose(1, 0, 2)
    return pl.pallas_call(
        functools.partial(_hyena_kernel, n=seq_len),
        out_shape=jax.ShapeDtypeStruct((n_seq * seq_len, C_CH), BF16),
        grid=(n_seq, C_CH // tc),
        in_specs=[xb(0), xb(1), xb(2),
                  pl.BlockSpec((3, SHORT_CONV, tc), lambda b, j: (0, 0, j)),
                  pl.BlockSpec((HYENA_ORDER, 2 * seq_len, tc), lambda b, j: (0, 0, j)),
                  pl.BlockSpec((HYENA_ORDER, tc), lambda b, j: (0, j)),
                  pl.BlockSpec((2 * seq_len, seq_len), lambda b, j: (0, 0)),
                  pl.BlockSpec((seq_len, 2 * seq_len), lambda b, j: (0, 0))],
        out_specs=pl.BlockSpec((seq_len, tc), lambda b, j: (b, j)),
        compiler_params=_cparams(),
        name=f"hyena_L{seq_len}",
    )(proj_r, proj_r, proj_r, cw, spec, skip, fwd, inv)


def _merge_kernel(oa_ref, ob_ref, oc_ref, wa_ref, wb_ref, wc_ref, ga_ref, gb_ref, gc_ref, o_ref,
                  wa_bf, wb_bf, wc_bf):
    @pl.when(pl.program_id(1) == 0)
    def _():
        wa_bf[...] = _bf(wa_ref[...])
        wb_bf[...] = _bf(wb_ref[...])
        wc_bf[...] = _bf(wc_ref[...])

    d = functools.partial(jnp.dot, preferred_element_type=F32)
    acc = _sigmoid(ga_ref[...]) * d(oa_ref[...], wa_bf[...])
    acc += _sigmoid(gb_ref[...]) * d(ob_ref[...], wb_bf[...])
    acc += _sigmoid(gc_ref[...]) * d(oc_ref[...], wc_bf[...])
    o_ref[...] = acc.astype(o_ref.dtype)


def _merge(o_a, o_b, o_c, w_a, w_b, w_c, proj_r, layer, tm=512, tn=512):
    t = o_a.shape[0]
    d = D_MODEL
    gate = lambda part: pl.BlockSpec((tm, tn), lambda j, i: (i, (R_GATES + part * d) // tn + j))
    act = lambda width: pl.BlockSpec((tm, width), lambda j, i: (i, 0))
    wgt = lambda width: _layer_spec((width, tn), lambda j, i: (0, j), layer)
    return pl.pallas_call(
        _merge_kernel,
        out_shape=jax.ShapeDtypeStruct((t, d), BF16),
        grid=(d // tn, t // tm),
        in_specs=[act(W_A), act(W_B), act(C_CH), wgt(W_A), wgt(W_B), wgt(C_CH),
                  gate(0), gate(1), gate(2)],
        out_specs=pl.BlockSpec((tm, tn), lambda j, i: (i, j)),
        scratch_shapes=[pltpu.VMEM((W_A, tn), BF16), pltpu.VMEM((W_B, tn), BF16), pltpu.VMEM((C_CH, tn), BF16)],
        compiler_params=_cparams(),
        name="merge",
    )(o_a, o_b, o_c, w_a, w_b, w_c, proj_r, proj_r, proj_r)


def _mm_resid_kernel(y_ref, w_ref, x_ref, gate_ref, o_ref, w_bf):
    @pl.when(pl.program_id(1) == 0)
    def _():
        w_bf[...] = _bf(w_ref[...])

    o_ref[...] = x_ref[...] + gate_ref[...] * jnp.dot(y_ref[...], w_bf[...], preferred_element_type=F32)


def _matmul_residual(y, w, x, mod4, gate_idx, layer, *, n_ctx_rows, rows_per_latent, tm=512, tn=512):
    t, kdim = y.shape
    n = w.shape[-1]
    assert y.dtype == BF16
    row = lambda i: _mod_row(i * tm, n_ctx_rows, rows_per_latent)
    return pl.pallas_call(
        _mm_resid_kernel,
        out_shape=jax.ShapeDtypeStruct((t, n), F32),
        grid=(n // tn, t // tm),
        in_specs=[pl.BlockSpec((tm, kdim), lambda j, i: (i, 0)),
                  _layer_spec((kdim, tn), lambda j, i: (0, j), layer),
                  pl.BlockSpec((tm, tn), lambda j, i: (i, j)),
                  pl.BlockSpec((None, None, 1, tn), lambda j, i: (row(i), gate_idx, 0, j))],
        out_specs=pl.BlockSpec((tm, tn), lambda j, i: (i, j)),
        scratch_shapes=[pltpu.VMEM((kdim, tn), BF16)],
        compiler_params=_cparams(),
        name="out_proj_residual",
    )(y, w, x, mod4)


def _router_kernel(x_ref, g_ref, scale_ref, shift_ref, wr_ref, br_ref, h_ref, ti_ref, tw_ref):
    x = x_ref[...]
    y = x * lax.rsqrt(jnp.mean(x * x, axis=-1, keepdims=True) + EPS)
    h = y * g_ref[...] * (1.0 + scale_ref[...]) + shift_ref[...]
    half = h.shape[1] // 2
    h_ref[...] = _pack_bf16_pair(h[:, :half], h[:, half:])
    logits = _dot_hl(h, wr_ref[...]) + br_ref[...]
    lane_i = lax.broadcasted_iota(jnp.int32, logits.shape, 1)
    lane = lane_i.astype(F32)
    neg = jnp.float32(-jnp.inf)
    cur = jnp.where(lane_i < N_EXPERTS, logits, neg)
    vals = []
    ti = jnp.zeros(logits.shape, F32)
    for kk in range(TOP_K):
        m = jnp.max(cur, axis=-1, keepdims=True)
        idx = jnp.min(jnp.where(cur == m, lane, float(LANES)), axis=-1, keepdims=True)
        ti = jnp.where(lane_i == kk, idx, ti)
        vals.append(m)
        cur = jnp.where(lane == idx, neg, cur)
    es = [jnp.exp(vk - vals[0]) for vk in vals]
    den = es[0] + es[1] + es[2] + es[3]
    tw = jnp.zeros(logits.shape, F32)
    for kk in range(TOP_K):
        tw = jnp.where(lane_i == kk, es[kk] / den, tw)
    ti_ref[...] = ti.astype(jnp.int32)
    tw_ref[...] = tw


def _router(x, g, mod4, shift_idx, scale_idx, w_router, b_router, *, n_ctx_rows, rows_per_latent, tm=256):
    t, d = x.shape
    row = lambda i: _mod_row(i * tm, n_ctx_rows, rows_per_latent)
    wr = jnp.pad(w_router, ((0, 0), (0, LANES - N_EXPERTS)))
    br = jnp.pad(b_router.reshape(1, -1), ((0, 0), (0, LANES - N_EXPERTS)))
    return pl.pallas_call(
        _router_kernel,
        out_shape=(jax.ShapeDtypeStruct((t, d // 2), jnp.uint32),
                   jax.ShapeDtypeStruct((t, LANES), jnp.int32),
                   jax.ShapeDtypeStruct((t, LANES), F32)),
        grid=(t // tm,),
        in_specs=[pl.BlockSpec((tm, d), lambda i: (i, 0)),
                  pl.BlockSpec((1, d), lambda i: (0, 0)),
                  pl.BlockSpec((None, None, 1, d), lambda i: (row(i), scale_idx, 0, 0)),
                  pl.BlockSpec((None, None, 1, d), lambda i: (row(i), shift_idx, 0, 0)),
                  pl.BlockSpec((d, LANES), lambda i: (0, 0)),
                  pl.BlockSpec((1, LANES), lambda i: (0, 0))],
        out_specs=(pl.BlockSpec((tm, d // 2), lambda i: (i, 0)),
                   pl.BlockSpec((tm, LANES), lambda i: (i, 0)),
                   pl.BlockSpec((tm, LANES), lambda i: (i, 0))),
        compiler_params=_cparams(),
        name="router",
    )(x, g.reshape(1, d), mod4, mod4, wr, br)


def _row_copy(src_hbm, dst_vmem, sem, src_row, dst_row):
    return pltpu.make_async_copy(src_hbm.at[pl.ds(src_row, 1), :], dst_vmem.at[pl.ds(dst_row, 1), :], sem)


def _gather_kernel(ta_ref, src_ref, nxt_ref, h_hbm, o_ref, buf, sem, *, tm):
    i = pl.program_id(0)
    last = pl.num_programs(0) - 1
    slot = i % 2

    unroll = 8
    stride = tm // unroll

    def fetch(idx_ref, s):
        def start(a, _):
            for b in range(unroll):
                r = b * stride + a
                _row_copy(h_hbm, buf.at[s], sem.at[s], idx_ref[0, r], r).start()
            return 0

        lax.fori_loop(0, stride, start, 0)

    def drain(s):
        def wait(a, _):
            for b in range(unroll):
                r = b * stride + a
                _row_copy(h_hbm, buf.at[s], sem.at[s], src_ref[0, r], r).wait()
            return 0

        lax.fori_loop(0, stride, wait, 0)
        lo, hi = _unpack_bf16_pair(buf[s])
        half = lo.shape[1]
        o_ref[:, :half] = lo.astype(o_ref.dtype)
        o_ref[:, half:] = hi.astype(o_ref.dtype)

    @pl.when(jnp.logical_and(i == 0, ta_ref[0] == 1))
    def _():
        fetch(src_ref, 0)

    next_active = jnp.logical_and(i < last, ta_ref[jnp.minimum(i + 1, last)] == 1)
    for s in (0, 1):
        @pl.when(jnp.logical_and(next_active, slot == 1 - s))
        def _():
            fetch(nxt_ref, s)

    for s in (0, 1):
        @pl.when(jnp.logical_and(ta_ref[i] == 1, slot == s))
        def _():
            drain(s)

    @pl.when(ta_ref[i] == 0)
    def _():
        o_ref[...] = jnp.zeros_like(o_ref)


def _moe_gather(h_packed, src_token, tile_active, n_rows):
    h = h_packed
    d = 2 * h.shape[1]
    tm = MOE_TM
    n_tiles = n_rows // tm
    src3 = src_token.reshape(n_tiles, 1, tm)
    return pl.pallas_call(
        functools.partial(_gather_kernel, tm=tm),
        out_shape=jax.ShapeDtypeStruct((n_rows, d), BF16),
        grid_spec=pltpu.PrefetchScalarGridSpec(
            num_scalar_prefetch=1,
            grid=(n_tiles,),
            in_specs=[pl.BlockSpec((None, 1, tm), lambda i, ta: (i, 0, 0), memory_space=pltpu.SMEM),
                      pl.BlockSpec((None, 1, tm), lambda i, ta: (jnp.minimum(i + 1, n_tiles - 1), 0, 0),
                                   memory_space=pltpu.SMEM),
                      pl.BlockSpec(memory_space=pl.ANY)],
            out_specs=pl.BlockSpec((tm, d), lambda i, ta: (i, 0)),
            scratch_shapes=[pltpu.VMEM((2, tm, d // 2), jnp.uint32), pltpu.SemaphoreType.DMA((2,))]),
        compiler_params=_cparams(dimension_semantics=("arbitrary",)),
        name="moe_gather",
    )(tile_active, src3, src3, h)


def _moe_up_kernel(te_ref, ta_ref, ts_ref, x_ref, wg_ref, wu_ref, bg_ref, bu_ref, rw_ref, o_ref):
    i = pl.program_id(1)

    @pl.when(ta_ref[i] == 1)
    def _():
        x = x_ref[...]
        gate = _dot(x, wg_ref[...]) + bg_ref[...]
        up = _dot(x, wu_ref[...]) + bu_ref[...]
        gate = jnp.minimum(gate, SWIGLU_LIMIT)
        up = jnp.clip(up, -SWIGLU_LIMIT, SWIGLU_LIMIT)
        act = (up + 1.0) * gate * _sigmoid(SWIGLU_ALPHA * gate)
        o_ref[...] = (act * rw_ref[...]).astype(o_ref.dtype)

    @pl.when(ta_ref[i] == 0)
    def _():
        o_ref[...] = jnp.zeros_like(o_ref)


def _moe_up(x_sorted, w_gu, b_gu, row_w, tile_expert, tile_active, tile_src, layer):
    p, d = x_sorted.shape
    tm, tf = MOE_TM, MOE_TF
    nf = D_FF // tf
    b3 = b_gu.reshape(N_EXPERTS, 1, 2 * D_FF)
    return pl.pallas_call(
        _moe_up_kernel,
        out_shape=jax.ShapeDtypeStruct((p, D_FF), BF16),
        grid_spec=pltpu.PrefetchScalarGridSpec(
            num_scalar_prefetch=3,
            grid=(nf, p // tm),
            in_specs=[pl.BlockSpec((tm, d), lambda j, i, te, ta, ts: (ts[i], 0)),
                      pl.BlockSpec((None, None, d, tf), lambda j, i, te, ta, ts: (layer, te[i], 0, j)),
                      pl.BlockSpec((None, None, d, tf), lambda j, i, te, ta, ts: (layer, te[i], 0, nf + j)),
                      pl.BlockSpec((None, 1, tf), lambda j, i, te, ta, ts: (te[i], 0, j)),
                      pl.BlockSpec((None, 1, tf), lambda j, i, te, ta, ts: (te[i], 0, nf + j)),
                      pl.BlockSpec((tm, 1), lambda j, i, te, ta, ts: (ts[i], 0))],
            out_specs=pl.BlockSpec((tm, tf), lambda j, i, te, ta, ts: (i, j)),
            scratch_shapes=[]),
        compiler_params=_cparams(),
        name="moe_up",
    )(tile_expert, tile_active, tile_src, x_sorted, w_gu, w_gu, b3, b3, row_w)


def _pack_bf16_pair(lo, hi):
    lo_bits = lax.bitcast_convert_type(_bf(lo).astype(F32), jnp.uint32)
    hi_bits = lax.bitcast_convert_type(_bf(hi).astype(F32), jnp.uint32)
    return hi_bits | (lo_bits >> 16)


def _unpack_bf16_pair(u):
    lo = lax.bitcast_convert_type(u << 16, F32)
    hi = lax.bitcast_convert_type(u & jnp.uint32(0xFFFF0000), F32)
    return lo, hi


def _moe_down_kernel(te_ref, ta_ref, ts_ref, a_ref, wlo_ref, whi_ref, blo_ref, bhi_ref, rw_ref, o_ref,
                     wlo_s, whi_s):
    i = pl.program_id(1)
    changed = jnp.logical_or(i == 0, te_ref[i] != te_ref[jnp.maximum(i - 1, 0)])

    @pl.when(changed)
    def _():
        wlo_s[...] = _bf(wlo_ref[...])
        whi_s[...] = _bf(whi_ref[...])

    @pl.when(ta_ref[i] == 1)
    def _():
        a = a_ref[...]
        rw = rw_ref[...]
        y_lo = jnp.dot(a, wlo_s[...], preferred_element_type=F32) + rw * blo_ref[...]
        y_hi = jnp.dot(a, whi_s[...], preferred_element_type=F32) + rw * bhi_ref[...]
        o_ref[...] = _pack_bf16_pair(y_lo, y_hi)

    @pl.when(ta_ref[i] == 0)
    def _():
        o_ref[...] = jnp.zeros_like(o_ref)


def _moe_down(act, w_down, b_down, row_w, tile_expert, tile_active, tile_src, layer, tn=1024):
    p, f = act.shape
    d = w_down.shape[-1]
    half = d // 2
    nh = half // tn
    tm = MOE_TM
    b3 = b_down.reshape(N_EXPERTS, 1, d)
    wspec = lambda off: pl.BlockSpec((None, None, f, tn), lambda j, i, te, ta, ts: (layer, te[i], 0, off + j))
    bspec = lambda off: pl.BlockSpec((None, 1, tn), lambda j, i, te, ta, ts: (te[i], 0, off + j))
    return pl.pallas_call(
        _moe_down_kernel,
        out_shape=jax.ShapeDtypeStruct((p, half), jnp.uint32),
        grid_spec=pltpu.PrefetchScalarGridSpec(
            num_scalar_prefetch=3,
            grid=(nh, p // tm),
            in_specs=[pl.BlockSpec((tm, f), lambda j, i, te, ta, ts: (ts[i], 0)),
                      wspec(0), wspec(nh), bspec(0), bspec(nh),
                      pl.BlockSpec((tm, 1), lambda j, i, te, ta, ts: (ts[i], 0))],
            out_specs=pl.BlockSpec((tm, tn), lambda j, i, te, ta, ts: (i, j)),
            scratch_shapes=[pltpu.VMEM((f, tn), BF16), pltpu.VMEM((f, tn), BF16)]),
        compiler_params=_cparams(),
        name="moe_down",
    )(tile_expert, tile_active, tile_src, act, w_down, w_down, b3, b3, row_w)


def _combine_kernel(dest_ref, nxt_ref, y_hbm, x_ref, gate_ref, fg_ref, o_ref, buf, sem, *, tm, final_norm):
    i = pl.program_id(0)
    last = pl.num_programs(0) - 1
    slot = i % 2

    def fetch(idx_ref, s):
        def start(r, _):
            for kk in range(TOP_K):
                _row_copy(y_hbm, buf.at[s, kk], sem.at[s], idx_ref[0, r * TOP_K + kk], r).start()
            return 0

        lax.fori_loop(0, tm, start, 0)

    def finish(s):
        def wait(r, _):
            for kk in range(TOP_K):
                _row_copy(y_hbm, buf.at[s, kk], sem.at[s], dest_ref[0, r * TOP_K + kk], r).wait()
            return 0

        lax.fori_loop(0, tm, wait, 0)
        parts = [_unpack_bf16_pair(buf[s, kk]) for kk in range(TOP_K)]
        half = buf.shape[-1]
        xs = []
        for side, cols in enumerate((slice(0, half), slice(half, 2 * half))):
            y = (parts[0][side] + parts[1][side]) + (parts[2][side] + parts[3][side])
            xs.append(x_ref[:, cols] + gate_ref[:, cols] * y)
        if final_norm:
            ms = (jnp.sum(xs[0] * xs[0], axis=-1, keepdims=True)
                  + jnp.sum(xs[1] * xs[1], axis=-1, keepdims=True)) / (2 * half)
            inv = lax.rsqrt(ms + EPS)
            xs = [xs[0] * inv * fg_ref[:, :half], xs[1] * inv * fg_ref[:, half:]]
        o_ref[:, :half] = xs[0]
        o_ref[:, half:] = xs[1]

    @pl.when(i == 0)
    def _():
        fetch(dest_ref, 0)

    for s in (0, 1):
        @pl.when(jnp.logical_and(i < last, slot == 1 - s))
        def _():
            fetch(nxt_ref, s)

    for s in (0, 1):
        @pl.when(slot == s)
        def _():
            finish(s)


def _moe_combine(y_sorted, dest, x, mod4, gate_idx, final_g, *, final_norm, n_ctx_rows, rows_per_latent):
    t, d = x.shape
    tm = COMBINE_TM
    row = lambda i: _mod_row(i * tm, n_ctx_rows, rows_per_latent)
    n_tiles = t // tm
    dest3 = dest.reshape(n_tiles, 1, tm * TOP_K)
    return pl.pallas_call(
        functools.partial(_combine_kernel, tm=tm, final_norm=final_norm),
        out_shape=jax.ShapeDtypeStruct((t, d), F32),
        grid=(n_tiles,),
        in_specs=[pl.BlockSpec((None, 1, tm * TOP_K), lambda i: (i, 0, 0), memory_space=pltpu.SMEM),
                  pl.BlockSpec((None, 1, tm * TOP_K), lambda i: (jnp.minimum(i + 1, n_tiles - 1), 0, 0),
                               memory_space=pltpu.SMEM),
                  pl.BlockSpec(memory_space=pl.ANY),
                  pl.BlockSpec((tm, d), lambda i: (i, 0)),
                  pl.BlockSpec((None, None, 1, d), lambda i: (row(i), gate_idx, 0, 0)),
                  pl.BlockSpec((1, d), lambda i: (0, 0))],
        out_specs=pl.BlockSpec((tm, d), lambda i: (i, 0)),
        scratch_shapes=[pltpu.VMEM((2, TOP_K, tm, d // 2), jnp.uint32), pltpu.SemaphoreType.DMA((2,))],
        compiler_params=_cparams(dimension_semantics=("arbitrary",)),
        name="moe_combine",
    )(dest3, dest3, y_sorted, x, mod4, final_g.reshape(1, d))


def _moe_plan(top_i, n_tiles):
    t = top_i.shape[0]
    tm = MOE_TM
    e_flat = top_i.reshape(-1)
    onehot = (e_flat[:, None] == jnp.arange(N_EXPERTS, dtype=jnp.int32)[None, :]).astype(jnp.int32)
    blk = 128
    within = jnp.cumsum(onehot.reshape(-1, blk, N_EXPERTS), axis=1)
    totals = within[:, -1, :]
    csum = (within + (jnp.cumsum(totals, axis=0) - totals)[:, None, :]).reshape(-1, N_EXPERTS)
    counts = csum[-1]
    rank = jnp.sum(onehot * csum, axis=1) - 1
    tiles_per = (counts + tm - 1) // tm
    tile_end = jnp.cumsum(tiles_per)
    group_row0 = (tile_end - tiles_per) * tm
    dest = group_row0[e_flat] + rank
    n_used = tile_end[-1]
    tile_ids = jnp.arange(n_tiles, dtype=jnp.int32)
    tile_expert = jnp.searchsorted(tile_end, tile_ids, side="right").astype(jnp.int32)
    tile_active = (tile_ids < n_used).astype(jnp.int32)
    last_expert = jnp.searchsorted(tile_end, n_used - 1, side="right").astype(jnp.int32)
    tile_expert = jnp.where(tile_active == 1, tile_expert, last_expert)
    tile_src = jnp.minimum(tile_ids, n_used - 1)
    return dest.astype(jnp.int32), tile_expert, tile_active, tile_src


def _moe(x, norm_g, mod4, p, l, final_g, *, final_norm, n_ctx_rows, rows_per_latent):
    t = x.shape[0]
    blk = dict(n_ctx_rows=n_ctx_rows, rows_per_latent=rows_per_latent)
    h2, top_i, top_w = _router(x, norm_g, mod4, 3, 4, p["w_router"][l], p["b_router"][l], **blk)
    top_i, top_w = top_i[:, :TOP_K], top_w[:, :TOP_K]
    n_rows = t * TOP_K + N_EXPERTS * MOE_TM
    dest, tile_expert, tile_active, tile_src = _moe_plan(top_i, n_rows // MOE_TM)
    slot_of_row = jnp.full((n_rows,), -1, jnp.int32).at[dest].set(jnp.arange(t * TOP_K, dtype=jnp.int32))
    real = slot_of_row >= 0
    src_token = jnp.where(real, slot_of_row // TOP_K, jnp.arange(n_rows, dtype=jnp.int32) % t)
    row_w = jnp.where(real, top_w.reshape(-1)[jnp.maximum(slot_of_row, 0)], 0.0).reshape(n_rows, 1)
    x_sorted = _moe_gather(h2, src_token, tile_active, n_rows)
    act = _moe_up(x_sorted, p["w_gu"], p["b_gu"][l], row_w, tile_expert, tile_active, tile_src, l)
    y_sorted = _moe_down(act, p["w_down"], p["b_down"][l], row_w, tile_expert, tile_active, tile_src, l)
    return _moe_combine(y_sorted, dest, x, mod4, 5, final_g, final_norm=final_norm, **blk)


def _trunk(x, cvec, p, final_g, caches, *, n_ctx, ctx_len, n_lat, lat_len):
    n_ctx_rows = n_ctx * ctx_len
    blk = dict(n_ctx_rows=n_ctx_rows, rows_per_latent=lat_len)
    rope = _rope_tables(lat_len)
    dft = {}
    for n in (ctx_len, lat_len):
        fwd, inv = _dft_tables(n)
        fwd_hi = _bf(fwd)
        dft[n] = (fwd_hi, _bf(fwd - fwd_hi.astype(F32)), _bf(inv))
    silu_c = jax.nn.silu(cvec)
    w_in_t = jnp.swapaxes(p["w_in"], 1, 2)
    outs = []
    for l in range(DEPTH):
        mod = _matmul(silu_c, p["w_mod"], n=N_MOD * D_MODEL, bias=p["b_mod"][l],
                      tm=16, tn=2048, tk=1024, name="modulation", layer=l)
        mod4 = mod.reshape(16, N_MOD, 1, D_MODEL)
        h = _adaln(x, p["norm1_g"][l], mod4, 0, 1, **blk)
        in_proj = functools.partial(_matmul_nt, h, w_in_t, layer=l)
        proj_a = in_proj(n=OFF_AB, row0=0, tm=512, tn=768, name="in_proj_a")
        ab = in_proj(n=LANES, row0=OFF_AB, tm=1024, tn=LANES, name="in_proj_ab")
        proj_r = in_proj(n=N_REST, row0=OFF_REST, tm=512, tn=768, name="in_proj_rest")

        ck, cv, s0f, s0b = caches[l]
        gates = _gdn_gates(ab, p["a_log"][l], p["dt_bias"][l])
        zeros_state = jnp.zeros((n_ctx, H_A, DK_A, DK_A), F32)
        gdn = functools.partial(_gdn, proj_a, gates, p["conv_a"][l], p["onorm_a"][l])
        oa_c, sf_c, sb_c = gdn(zeros_state, zeros_state, n_seq=n_ctx, seq_len=ctx_len, row0=0)
        oa_l, _, _ = gdn(s0f, s0b, n_seq=n_lat, seq_len=lat_len, row0=n_ctx_rows)

        lam_init = 0.8 - 0.6 * math.exp(-0.3 * l)
        attn = functools.partial(_attention, proj_r, p["lam"][l], p["subln_b"][l], lam_init)
        ob_c = attn(n_seq=n_ctx, seq_len=ctx_len, row0=0)
        ob_l = attn(n_seq=n_lat, seq_len=lat_len, row0=n_ctx_rows, rope=rope, cache=(ck, cv))

        oc = []
        for n_seq, n, row0 in ((n_ctx, ctx_len, 0), (n_lat, lat_len, n_ctx_rows)):
            fwd_hi, fwd_lo, inv = dft[n]
            spec = _hyena_filters(n, p["filt_w1"][l], p["filt_b1"][l], p["filt_freq"][l],
                                  p["filt_w2"][l], p["filt_b2"][l], p["filt_w3"][l], fwd_hi, fwd_lo)
            oc.append(_hyena(proj_r, p["conv_c"][l], spec, p["filt_skip"][l], fwd_hi, inv,
                             n_seq=n_seq, seq_len=n, row0=row0))

        o_a = jnp.concatenate([oa_c, oa_l], axis=0)
        o_b = jnp.concatenate([ob_c, ob_l], axis=0)
        o_c = jnp.concatenate(oc, axis=0)
        merged = _merge(o_a, o_b, o_c, p["w_br_a"], p["w_br_b"], p["w_br_c"], proj_r, l)
        x = _matmul_residual(merged, p["w_out"], x, mod4, 2, l, **blk)
        x = _moe(x, p["norm2_g"][l], mod4, p, l, final_g, final_norm=(l == DEPTH - 1), **blk)

        kv = proj_r[:n_ctx_rows, R_KB:R_XC].reshape(n_ctx, ctx_len, 2, H_B, DV_B)
        outs.append((kv[:, :, 0], kv[:, :, 1], sf_c, sb_c))
    return x, outs


def kernel(x_prompt, x_sample, cache_k, cache_v, state_fwd, state_bwd, c, c_ctx, norm1_g, norm2_g, final_g, w_mod, b_mod, w_in, conv_a, a_log, dt_bias, onorm_a, lam, subln_b, conv_c, filt_w1, filt_b1, filt_freq, filt_w2, filt_b2, filt_w3, filt_skip, w_br_a, w_br_b, w_br_c, w_out, w_router, b_router, w_gu, b_gu, w_down, b_down):
    p = dict(norm1_g=norm1_g, norm2_g=norm2_g, w_mod=w_mod, b_mod=b_mod, w_in=w_in, conv_a=conv_a,
             a_log=a_log, dt_bias=dt_bias, onorm_a=onorm_a, lam=lam, subln_b=subln_b, conv_c=conv_c,
             filt_w1=filt_w1, filt_b1=filt_b1, filt_freq=filt_freq, filt_w2=filt_w2,
             filt_b2=filt_b2, filt_w3=filt_w3, filt_skip=filt_skip, w_br_a=w_br_a,
             w_br_b=w_br_b, w_br_c=w_br_c, w_out=w_out, w_router=w_router, b_router=b_router,
             w_gu=w_gu, b_gu=b_gu, w_down=w_down, b_down=b_down)
    n_ctx, ctx_len, d = x_prompt.shape
    n_lat, lat_len, _ = x_sample.shape
    past = cache_k.shape[2]
    x = jnp.concatenate([x_prompt.reshape(n_ctx * ctx_len, d), x_sample.reshape(n_lat * lat_len, d)], axis=0)
    cvec = jnp.concatenate([c_ctx[None, :], c, jnp.zeros((16 - 1 - n_lat, d), F32)], axis=0)
    caches = [(cache_k[:, l].reshape(n_lat, past, W_B), cache_v[:, l].reshape(n_lat, past, W_B),
               state_fwd[:, l], state_bwd[:, l]) for l in range(DEPTH)]
    y, outs = _trunk(x, cvec, p, final_g, caches, n_ctx=n_ctx, ctx_len=ctx_len, n_lat=n_lat, lat_len=lat_len)
    y_prompt = y[:n_ctx * ctx_len].reshape(n_ctx, ctx_len, d)
    y_sample = y[n_ctx * ctx_len:].reshape(n_lat, lat_len, d)
    stack = lambda idx: jnp.stack([o[idx] for o in outs], axis=1)
    return (y_prompt, y_sample, stack(0), stack(1), stack(2), stack(3))
```

```python
import functools
import math

import jax
import jax.numpy as jnp
from jax import lax
from jax.experimental import pallas as pl
from jax.experimental.pallas import tpu as pltpu

F32 = jnp.float32
BF16 = jnp.bfloat16

D_MODEL = 4096
DEPTH = 2
GRID_W = 64
H_A = 12
DK_A = 128
W_A = H_A * DK_A
SHORT_CONV = 3
H_B = 6
DK_B = 128
DV_B = 2 * DK_B
W_B = H_B * DV_B
ROPE_BASE = 10000.0
C_CH = 1024
HYENA_ORDER = 2
FILT_BANDS = 16
FILT_EMB = 1 + 2 * FILT_BANDS
FILT_HIDDEN = 64
HYENA_FAST_DECAY = 0.3
HYENA_SLOW_DECAY = 1.5
HYENA_TARGET = 1e-2
N_EXPERTS = 32
TOP_K = 4
D_FF = 1024
SWIGLU_LIMIT = 7.0
SWIGLU_ALPHA = 1.702
N_MOD = 6
EPS = 1e-6

OFF_AB = 4 * W_A
OFF_REST = OFF_AB + 4 * H_A
R_QB, R_KB, R_VB = 0, W_B, 2 * W_B
R_XC = 3 * W_B
R_GATES = R_XC + 3 * C_CH
N_REST = R_GATES + 3 * D_MODEL

LANES = 128
VMEM_LIMIT = 56 * 1024 * 1024
GDN_CHUNK = 256
GDN_HEADS = 4
ATT_QBLOCK = 256
MOE_TM = 512
MOE_TF = 512
COMBINE_TM = 128


def _cparams(**kw):
    return pltpu.CompilerParams(vmem_limit_bytes=VMEM_LIMIT, **kw)


def _bf(x):
    return x.astype(BF16)


def _dot(a, b):
    return jnp.dot(_bf(a), _bf(b), preferred_element_type=F32)


def _dot_nt(a, b):
    return lax.dot_general(_bf(a), _bf(b), (((1,), (1,)), ((), ())), preferred_element_type=F32)


def _dot_tn(a, b):
    return lax.dot_general(_bf(a), _bf(b), (((0,), (0,)), ((), ())), preferred_element_type=F32)


def _split2(x):
    hi = _bf(x)
    lo = _bf(x - hi.astype(F32))
    return hi, lo


def _split3(x):
    hi = _bf(x)
    r = x - hi.astype(F32)
    mid = _bf(r)
    lo = _bf(r - mid.astype(F32))
    return hi, mid, lo


def _dot_hl(a, b):
    ah, al = _split2(a)
    bh, bl = _split2(b)
    d = functools.partial(jnp.dot, preferred_element_type=F32)
    return d(ah, bh) + (d(ah, bl) + d(al, bh))


def _dot_exact_lhs(m01, x):
    m = _bf(m01)
    h, mi, lo = _split3(x)
    d = functools.partial(jnp.dot, preferred_element_type=F32)
    return d(m, h) + (d(m, mi) + d(m, lo))


def _sigmoid(x):
    return 0.5 * jnp.tanh(0.5 * x) + 0.5


def _silu(x):
    return x * _sigmoid(x)


def _softplus(x):
    return jnp.maximum(x, 0.0) + jnp.log(1.0 + jnp.exp(-jnp.abs(x)))


def _mod_row(row_start, n_ctx_rows, rows_per_latent):
    return jnp.where(row_start < n_ctx_rows, 0, 1 + (row_start - n_ctx_rows) // rows_per_latent)


def _mm_kernel(x_ref, w_ref, o_ref, acc_ref, *, nk):
    k = pl.program_id(2)

    @pl.when(k == 0)
    def _():
        acc_ref[...] = jnp.zeros_like(acc_ref)

    acc_ref[...] += _dot(x_ref[...], w_ref[...])

    @pl.when(k == nk - 1)
    def _():
        o_ref[...] = acc_ref[...].astype(o_ref.dtype)


def _mm_bias_kernel(x_ref, w_ref, b_ref, o_ref, acc_ref, *, nk):
    k = pl.program_id(2)

    @pl.when(k == 0)
    def _():
        acc_ref[...] = jnp.zeros_like(acc_ref)

    acc_ref[...] += _dot(x_ref[...], w_ref[...])

    @pl.when(k == nk - 1)
    def _():
        o_ref[...] = (acc_ref[...] + b_ref[...]).astype(o_ref.dtype)


def _layer_spec(block, index_map, layer):
    if layer is None:
        return pl.BlockSpec(block, index_map)
    return pl.BlockSpec((None,) + tuple(block), lambda *a: (layer,) + tuple(index_map(*a)))


def _matmul(x, w, *, n, col0=0, bias=None, tm, tn, tk, out_dtype=F32, name, layer=None):
    m, kdim = x.shape
    assert m % tm == 0 and n % tn == 0 and kdim % tk == 0 and col0 % tn == 0
    nk = kdim // tk
    cb = col0 // tn
    in_specs = [pl.BlockSpec((tm, tk), lambda i, j, k: (i, k)),
                _layer_spec((tk, tn), lambda i, j, k: (k, j + cb), layer)]
    args = [x, w]
    if bias is None:
        body = functools.partial(_mm_kernel, nk=nk)
    else:
        body = functools.partial(_mm_bias_kernel, nk=nk)
        in_specs.append(pl.BlockSpec((1, tn), lambda i, j, k: (0, j)))
        args.append(bias.reshape(1, n))
    return pl.pallas_call(
        body,
        out_shape=jax.ShapeDtypeStruct((m, n), out_dtype),
        grid=(m // tm, n // tn, nk),
        in_specs=in_specs,
        out_specs=pl.BlockSpec((tm, tn), lambda i, j, k: (i, j)),
        scratch_shapes=[pltpu.VMEM((tm, tn), F32)],
        compiler_params=_cparams(),
        name=name,
    )(*args)


def _mm_nt_kernel(x_ref, w_ref, o_ref, w_bf):
    @pl.when(pl.program_id(1) == 0)
    def _():
        w_bf[...] = _bf(w_ref[0])

    o_ref[...] = lax.dot_general(x_ref[...], w_bf[...], (((1,), (1,)), ((), ())),
                                 preferred_element_type=F32).astype(o_ref.dtype)


def _matmul_nt(x, w_t, *, n, row0, layer, tm, tn, name, out_dtype=F32):
    m, kdim = x.shape
    assert m % tm == 0 and n % tn == 0 and row0 % 8 == 0 and tn % 8 == 0
    assert row0 + n <= w_t.shape[1] and x.dtype == BF16
    return pl.pallas_call(
        _mm_nt_kernel,
        out_shape=jax.ShapeDtypeStruct((m, n), out_dtype),
        grid=(n // tn, m // tm),
        in_specs=[pl.BlockSpec((tm, kdim), lambda j, i: (i, 0)),
                  pl.BlockSpec((pl.Element(1), pl.Element(tn), pl.Element(kdim)),
                               lambda j, i: (layer, (row0 // 8 + j * (tn // 8)) * 8, 0))],
        out_specs=pl.BlockSpec((tm, tn), lambda j, i: (i, j)),
        scratch_shapes=[pltpu.VMEM((tn, kdim), BF16)],
        compiler_params=_cparams(),
        name=name,
    )(x, w_t)


def _adaln_kernel(x_ref, g_ref, scale_ref, shift_ref, o_ref):
    x = x_ref[...]
    y = x * lax.rsqrt(jnp.mean(x * x, axis=-1, keepdims=True) + EPS)
    o_ref[...] = (y * g_ref[...] * (1.0 + scale_ref[...]) + shift_ref[...]).astype(o_ref.dtype)


def _adaln(x, g, mod4, shift_idx, scale_idx, *, n_ctx_rows, rows_per_latent, tm=256):
    t, d = x.shape
    row = lambda i: _mod_row(i * tm, n_ctx_rows, rows_per_latent)
    return pl.pallas_call(
        _adaln_kernel,
        out_shape=jax.ShapeDtypeStruct((t, d), BF16),
        grid=(t // tm,),
        in_specs=[pl.BlockSpec((tm, d), lambda i: (i, 0)),
                  pl.BlockSpec((1, d), lambda i: (0, 0)),
                  pl.BlockSpec((None, None, 1, d), lambda i: (row(i), scale_idx, 0, 0)),
                  pl.BlockSpec((None, None, 1, d), lambda i: (row(i), shift_idx, 0, 0))],
        out_specs=pl.BlockSpec((tm, d), lambda i: (i, 0)),
        compiler_params=_cparams(),
        name="adaln",
    )(x, g.reshape(1, d), mod4, mod4)


def _gdn_gate_kernel(ab_ref, alog_ref, dtb_ref, o_ref):
    ab = ab_ref[...]
    tm = ab.shape[0]
    lane = lax.broadcasted_iota(jnp.int32, ab.shape, 1)
    rows = lax.broadcasted_iota(jnp.int32, (tm, tm), 0)
    cols = lax.broadcasted_iota(jnp.int32, (tm, tm), 1)
    g = -jnp.exp(alog_ref[...]) * _softplus(ab + dtb_ref[...])
    prefix = _dot_exact_lhs((cols <= rows).astype(F32), g)
    suffix = _dot_exact_lhs((cols >= rows).astype(F32), g)
    gcum = jnp.where(lane < H_A, prefix, suffix)
    o_ref[...] = jnp.where(lane < 2 * H_A, gcum, _sigmoid(ab))


def _gdn_gates(ab, a_log, dt_bias):
    t = ab.shape[0]
    tm = GDN_CHUNK
    pad = lambda v: jnp.pad(v.reshape(1, 2 * H_A), ((0, 0), (0, LANES - 2 * H_A)))
    return pl.pallas_call(
        _gdn_gate_kernel,
        out_shape=jax.ShapeDtypeStruct((t, LANES), F32),
        grid=(t // tm,),
        in_specs=[pl.BlockSpec((tm, LANES), lambda i: (i, 0)),
                  pl.BlockSpec((1, LANES), lambda i: (0, 0)),
                  pl.BlockSpec((1, LANES), lambda i: (0, 0))],
        out_specs=pl.BlockSpec((tm, LANES), lambda i: (i, 0)),
        compiler_params=_cparams(),
        name="gdn_gates",
    )(ab, pad(a_log), pad(dt_bias))


def _unit_tri_inverse(mats, rows, cols, dot):
    n = mats[0].shape[0]
    eye = (rows == cols).astype(F32)
    same = lambda s: (rows // s) == (cols // s)
    dps = [jnp.where(same(16), a, 0.0) for a in mats]
    ts = [eye - d for d in dps]
    for _ in range(3):
        dbs = [_bf(d) for d in dps]
        dps = [dot(d, d) for d in dbs]
        ts = [t + dot(t, d) for t, d in zip(ts, dps)]
    s = 16
    while s < n:
        mask = same(2 * s) & jnp.logical_not(same(s))
        tbs = [_bf(t) for t in ts]
        lts = [dot(jnp.where(mask, a, 0.0), tb) for a, tb in zip(mats, tbs)]
        ts = [t - dot(tb, lt) for t, tb, lt in zip(ts, tbs, lts)]
        s *= 2
    return ts


def _gdn_kernel(q_ref, k_ref, v_ref, z_ref, gate_ref, cw_ref, og_ref, s0f_ref, s0b_ref,
                o_ref, sf_ref, sb_ref, qs, ks, vs, of_s, ob_s, *, seq_len, chunk, heads):
    h0 = pl.program_id(1) * heads
    n_chunks = seq_len // chunk
    width = heads * DK_A
    head = lambda j: slice(j * DK_A, (j + 1) * DK_A)

    pos = lax.broadcasted_iota(jnp.int32, (seq_len, width), 0)

    def conv_silu(x_ref, w):
        x = x_ref[...]
        prev = jnp.where(pos == 0, 0.0, pltpu.roll(x, 1, 0))
        nxt = jnp.where(pos == seq_len - 1, 0.0, pltpu.roll(x, seq_len - 1, 0))
        return _silu(prev * w[0:1, :] + x * w[1:2, :] + nxt * w[2:3, :])

    def l2n(x):
        return x * lax.rsqrt(jnp.sum(x * x, axis=-1, keepdims=True) + 1e-6)

    qc = conv_silu(q_ref, cw_ref[0])
    kc = conv_silu(k_ref, cw_ref[1])
    for j in range(heads):
        qs[:, head(j)] = l2n(qc[:, head(j)]) * (DK_A ** -0.5)
        ks[:, head(j)] = l2n(kc[:, head(j)])
    vs[...] = conv_silu(v_ref, cw_ref[2])

    rows = lax.broadcasted_iota(jnp.int32, (chunk, chunk), 0)
    cols = lax.broadcasted_iota(jnp.int32, (chunk, chunk), 1)
    lane = lax.broadcasted_iota(jnp.int32, (chunk, LANES), 1)

    chains = [(j, d) for j in range(heads) for d in (0, 1)]
    masks = {0: (cols <= rows, cols < rows),
             1: (cols >= rows, cols > rows)}
    out_refs = {0: of_s, 1: ob_s}

    def body(i, states):
        r0s = {0: pl.multiple_of(i * chunk, chunk), 1: pl.multiple_of((n_chunks - 1 - i) * chunk, chunk)}
        gates = {d: gate_ref[pl.ds(r0s[d], chunk), :] for d in (0, 1)}
        pick = lambda d, idx: jnp.sum(jnp.where(lane == idx, gates[d], 0.0), axis=-1, keepdims=True)
        qkv = [tuple(s[pl.ds(r0s[d], chunk), head(j)] for s in (qs, ks, vs)) for j, d in chains]
        gcum = [jnp.broadcast_to(pick(d, d * H_A + h0 + j), (chunk, LANES)) for j, d in chains]
        beta = [pick(d, (2 + d) * H_A + h0 + j) for j, d in chains]
        last = {0: chunk - 1, 1: 0}
        g_last = [gcm[last[d]:last[d] + 1, :] for (j, d), gcm in zip(chains, gcum)]
        decay = []
        for (j, d), gcm in zip(chains, gcum):
            gc = jnp.concatenate([gcm] * (chunk // LANES), axis=1)
            incl = masks[d][0]
            decay.append(jnp.where(incl, jnp.exp(jnp.where(incl, gc - gc.T, 0.0)), 0.0))
        kk = [_dot_nt(k, k) for q, k, v in qkv]
        a = [jnp.where(masks[d][1], b * x * dc, 0.0) for (j, d), b, x, dc in zip(chains, beta, kk, decay)]
        t = _unit_tri_inverse(a, rows, cols, _dot)
        rhs = [jnp.concatenate([v * b, k * b * jnp.exp(gcm)], axis=-1)
               for (q, k, v), b, gcm in zip(qkv, beta, gcum)]
        uw = [_dot(ti, r) for ti, r in zip(t, rhs)]
        qk = [_dot_nt(q, k) * dc for (q, k, v), dc in zip(qkv, decay)]
        ws = [_dot(jnp.concatenate([x[:, DK_A:], q * jnp.exp(gcm)], axis=0), s)
              for x, (q, k, v), gcm, s in zip(uw, qkv, gcum, states)]
        v_new = [x[:, :DK_A] - y[:chunk] for x, y in zip(uw, ws)]
        o = [y[chunk:] + _dot(m, vn) for y, m, vn in zip(ws, qk, v_new)]
        for (j, d), oi in zip(chains, o):
            out_refs[d][pl.ds(r0s[d], chunk), head(j)] = oi
        return tuple(s * jnp.exp(gl[:, 0:1]) + _dot_tn(k * jnp.exp(gl - gcm), vn)
                     for s, gl, (q, k, v), gcm, vn in zip(states, g_last, qkv, gcum, v_new))

    init = tuple((s0f_ref, s0b_ref)[d][j] for j, d in chains)
    final = lax.fori_loop(0, n_chunks, body, init)
    for (j, d), s in zip(chains, final):
        (sf_ref, sb_ref)[d][j] = s
    for j in range(heads):
        o = of_s[:, head(j)] + ob_s[:, head(j)]
        o = o * lax.rsqrt(jnp.mean(o * o, axis=-1, keepdims=True) + EPS) * og_ref[...]
        o_ref[:, head(j)] = (o * _silu(z_ref[:, head(j)])).astype(o_ref.dtype)


def _gdn(proj_a, gates, conv_w, onorm_g, s0f, s0b, *, n_seq, seq_len, row0, heads=GDN_HEADS):
    assert row0 % seq_len == 0 and seq_len % GDN_CHUNK == 0 and H_A % heads == 0
    rb = row0 // seq_len
    ng = H_A // heads
    width = heads * DK_A
    cw = conv_w.reshape(SHORT_CONV, 3, W_A).transpose(1, 0, 2)
    tok = lambda part: pl.BlockSpec((seq_len, width), lambda b, h: (b + rb, part * ng + h))
    st = pl.BlockSpec((None, heads, DK_A, DK_A), lambda b, h: (b, h, 0, 0))
    body = functools.partial(_gdn_kernel, seq_len=seq_len, chunk=GDN_CHUNK, heads=heads)
    return pl.pallas_call(
        body,
        out_shape=(jax.ShapeDtypeStruct((n_seq * seq_len, W_A), BF16),
                   jax.ShapeDtypeStruct((n_seq, H_A, DK_A, DK_A), F32),
                   jax.ShapeDtypeStruct((n_seq, H_A, DK_A, DK_A), F32)),
        grid=(n_seq, ng),
        in_specs=[tok(0), tok(1), tok(2), tok(3),
                  pl.BlockSpec((seq_len, LANES), lambda b, h: (b + rb, 0)),
                  pl.BlockSpec((3, SHORT_CONV, width), lambda b, h: (0, 0, h)),
                  pl.BlockSpec((1, DK_A), lambda b, h: (0, 0)),
                  st, st],
        out_specs=(pl.BlockSpec((seq_len, width), lambda b, h: (b, h)), st, st),
        scratch_shapes=[pltpu.VMEM((seq_len, width), F32) for _ in range(5)],
        compiler_params=_cparams(),
        name=f"gdn_L{seq_len}",
    )(proj_a, proj_a, proj_a, proj_a, gates, cw, onorm_g.reshape(1, DK_A), s0f, s0b)


def _rope(x, cos, sin_signed, lane):
    rot = jnp.where((lane % 64) < 32, pltpu.roll(x, LANES - 32, 1), pltpu.roll(x, 32, 1))
    return x * cos + rot * sin_signed


def _attn_kernel(*refs, seq_len, qblock, use_rope, n_cache, lam_init):
    it = iter(refs)
    q_ref, k_ref, v_ref, lam_ref, g_ref = next(it), next(it), next(it), next(it), next(it)
    cos_ref = sin_ref = ck_ref = cv_ref = None
    if use_rope:
        cos_ref, sin_ref = next(it), next(it)
    if n_cache:
        ck_ref, cv_ref = next(it), next(it)
    o_ref = next(it)
    ks = next(it)

    lam = lam_ref[...]
    lam_full = (jnp.exp(jnp.sum(lam[0:1] * lam[1:2], axis=-1, keepdims=True))
                - jnp.exp(jnp.sum(lam[2:3] * lam[3:4], axis=-1, keepdims=True)) + lam_init)
    scale = DK_B ** -0.5
    lane = lax.broadcasted_iota(jnp.int32, (seq_len, DK_B), 1) if use_rope else None
    lane_q = lax.broadcasted_iota(jnp.int32, (qblock, DK_B), 1) if use_rope else None
    for r in range(2):
        kr = k_ref[:, r * DK_B:(r + 1) * DK_B]
        if use_rope:
            kr = _rope(kr, cos_ref[...], sin_ref[...], lane)
        ks[r] = _bf(kr)
    v = _bf(v_ref[...])
    for qb in range(seq_len // qblock):
        sl = slice(qb * qblock, (qb + 1) * qblock)
        probs = []
        for r in range(2):
            qr = q_ref[sl, r * DK_B:(r + 1) * DK_B]
            if use_rope:
                qr = _rope(qr, cos_ref[sl, :], sin_ref[sl, :], lane_q)
            s = _dot_nt(qr, ks[r]) * scale
            m = jnp.max(s, axis=-1, keepdims=True)
            if n_cache:
                sc = _dot_nt(qr, ck_ref[:, r * DK_B:(r + 1) * DK_B]) * scale
                m = jnp.maximum(m, jnp.max(sc, axis=-1, keepdims=True))
                ec = jnp.exp(sc - m)
            e = jnp.exp(s - m)
            den = jnp.sum(e, axis=-1, keepdims=True)
            if n_cache:
                den = den + jnp.sum(ec, axis=-1, keepdims=True)
                probs.append((e / den, ec / den))
            else:
                probs.append((e / den, None))
        o = _dot(probs[0][0] - lam_full * probs[1][0], v)
        if n_cache:
            o = o + _dot(probs[0][1] - lam_full * probs[1][1], cv_ref[...])
        o = o * lax.rsqrt(jnp.mean(o * o, axis=-1, keepdims=True) + 1e-5) * g_ref[...]
        o_ref[sl, :] = (o * (1.0 - lam_init)).astype(o_ref.dtype)


def _attention(proj_r, lam, subln_g, lam_init, *, n_seq, seq_len, row0, rope=None, cache=None):
    assert row0 % seq_len == 0
    rb = row0 // seq_len
    nh = H_B
    blk = lambda part: pl.BlockSpec((seq_len, DV_B), lambda b, h: (b + rb, part * nh + h))
    in_specs = [blk(0), blk(1), blk(2),
                pl.BlockSpec((4, DK_B), lambda b, h: (0, 0)),
                pl.BlockSpec((1, DV_B), lambda b, h: (0, 0))]
    args = [proj_r, proj_r, proj_r, lam, subln_g.reshape(1, DV_B)]
    if rope is not None:
        in_specs += [pl.BlockSpec((seq_len, DK_B), lambda b, h: (0, 0))] * 2
        args += list(rope)
    n_cache = 0
    if cache is not None:
        ck, cv = cache
        n_cache = ck.shape[1]
        in_specs += [pl.BlockSpec((None, n_cache, DV_B), lambda b, h: (b, 0, h))] * 2
        args += [ck, cv]
    body = functools.partial(_attn_kernel, seq_len=seq_len, qblock=min(ATT_QBLOCK, seq_len),
                             use_rope=rope is not None, n_cache=n_cache, lam_init=lam_init)
    return pl.pallas_call(
        body,
        out_shape=jax.ShapeDtypeStruct((n_seq * seq_len, W_B), BF16),
        grid=(n_seq, nh),
        in_specs=in_specs,
        out_specs=pl.BlockSpec((seq_len, DV_B), lambda b, h: (b, h)),
        scratch_shapes=[pltpu.VMEM((2, seq_len, DK_B), BF16)],
        compiler_params=_cparams(),
        name=f"diff_attn_L{seq_len}",
    )(*args)


def _rope_tables(n_tok):
    rows = n_tok // GRID_W
    row = jnp.repeat(jnp.arange(rows), GRID_W)
    col = jnp.tile(jnp.arange(GRID_W), rows)
    half = DK_B // 2
    inv = ROPE_BASE ** (-jnp.arange(0, half, 2, dtype=F32) / half)
    ang = jnp.stack([row, col], axis=-1).astype(F32)[..., None] * inv
    cos, sin = jnp.cos(ang), jnp.sin(ang)
    cos_t = jnp.concatenate([cos, cos], axis=-1).reshape(n_tok, DK_B)
    sin_t = jnp.concatenate([-sin, sin], axis=-1).reshape(n_tok, DK_B)
    return cos_t, sin_t


def _dft_tables(n):
    f = jnp.arange(n, dtype=jnp.int32)[:, None]
    t = jnp.arange(n, dtype=jnp.int32)[None, :]

    def cos_sin(m):
        ang = ((f * m) % (2 * n)).astype(F32) * (math.pi / n)
        return jnp.cos(ang), jnp.sin(ang)

    g = 32
    ca, sa = cos_sin(jnp.arange(0, n, g, dtype=jnp.int32)[None, :])
    cb, sb = cos_sin(jnp.arange(g, dtype=jnp.int32)[None, :])
    ca, sa, cb, sb = ca[:, :, None], sa[:, :, None], cb[:, None, :], sb[:, None, :]
    cos = (ca * cb - sa * sb).reshape(n, n)
    sin = (sa * cb + ca * sb).reshape(n, n)
    nyq = jnp.where(t % 2 == 0, 1.0, -1.0).astype(F32)
    fwd_im = jnp.where(f == 0, nyq, -sin)
    fwd = jnp.concatenate([cos, fwd_im], axis=0)
    wgt = jnp.where(f == 0, 1.0, 2.0).astype(F32) / (2 * n)
    inv_re = (wgt * cos).T
    inv_im = jnp.where(f == 0, nyq / (2 * n), -wgt * sin).T
    inv = jnp.concatenate([inv_re, inv_im], axis=1)
    return fwd, inv


def _spec_mul(u, s, n, row):
    ur, ui, sr, si = u[:n], u[n:], s[:n], s[n:]
    first = row == 0
    yr = ur * sr - jnp.where(first, 0.0, ui * si)
    yi = jnp.where(first, ui * si, ur * si + ui * sr)
    return yr, yi


def _hyena_filter_kernel(z_ref, w1_ref, b1_ref, fr_ref, w2_ref, b2_ref, w3_ref, win_ref,
                         fh_ref, fl_ref, o_ref, *, n):
    fr = fr_ref[...]
    h = jnp.sin(fr * (_dot_hl(z_ref[...], w1_ref[...]) + b1_ref[...]))
    h = jnp.sin(fr * (_dot_hl(h, w2_ref[...]) + b2_ref[...]))
    win = win_ref[...]
    row = lax.broadcasted_iota(jnp.int32, win.shape, 0)
    fwd_hi, fwd_lo = fh_ref[...], fl_ref[...]

    def dft(x):
        xh, xl = _split2(x)
        d = functools.partial(jnp.dot, preferred_element_type=F32)
        return d(fwd_hi, xh) + (d(fwd_hi, xl) + d(fwd_lo, xh))

    for o in range(HYENA_ORDER):
        hf = _dot_hl(h, w3_ref[2 * o]) * win
        hb = jnp.where(row == 0, 0.0, _dot_hl(h, w3_ref[2 * o + 1]) * win)
        p, q = dft(hf), dft(hb)
        o_ref[o, :n, :] = p[:n] + q[:n]
        o_ref[o, n:, :] = jnp.where(row == 0, p[n:] + q[n:], p[n:] - q[n:])


def _hyena_filters(n, w1, b1, freq, w2, b2, w3, fwd_hi, fwd_lo, tc=256):
    t = jnp.linspace(0.0, 1.0, n, dtype=F32)[:, None]
    wpos = 2.0 * math.pi * jnp.arange(n, dtype=F32)[:, None] / n
    f = jnp.linspace(1e-4, FILT_BANDS - 1, FILT_BANDS, dtype=F32)
    z = jnp.concatenate([t, jnp.cos(wpos * f), -jnp.sin(wpos * f)], axis=-1)
    z = jnp.pad(z, ((0, 0), (0, LANES - FILT_EMB)))
    w1p = jnp.pad(w1, ((0, LANES - FILT_EMB), (0, 0)))
    max_decay = math.log(HYENA_TARGET) / HYENA_FAST_DECAY
    min_decay = math.log(HYENA_TARGET) / HYENA_SLOW_DECAY
    deltas = jnp.linspace(min_decay, max_decay, C_CH, dtype=F32)
    window = jnp.exp(-t * jnp.abs(deltas))
    w3r = w3.reshape(FILT_HIDDEN, 2 * HYENA_ORDER, C_CH).transpose(1, 0, 2)
    full = lambda shape: pl.BlockSpec(shape, lambda j: (0,) * len(shape))
    return pl.pallas_call(
        functools.partial(_hyena_filter_kernel, n=n),
        out_shape=jax.ShapeDtypeStruct((HYENA_ORDER, 2 * n, C_CH), F32),
        grid=(C_CH // tc,),
        in_specs=[full((n, LANES)), full((LANES, FILT_HIDDEN)), full((1, FILT_HIDDEN)),
                  full((1, FILT_HIDDEN)), full((FILT_HIDDEN, FILT_HIDDEN)), full((1, FILT_HIDDEN)),
                  pl.BlockSpec((2 * HYENA_ORDER, FILT_HIDDEN, tc), lambda j: (0, 0, j)),
                  pl.BlockSpec((n, tc), lambda j: (0, j)),
                  full((2 * n, n)), full((2 * n, n))],
        out_specs=pl.BlockSpec((HYENA_ORDER, 2 * n, tc), lambda j: (0, 0, j)),
        compiler_params=_cparams(),
        name=f"hyena_filter_L{n}",
    )(z, w1p, b1.reshape(1, -1), freq.reshape(1, -1), w2, b2.reshape(1, -1), w3r, window,
      fwd_hi, fwd_lo)


def _hyena_kernel(x1_ref, x2_ref, v_ref, cw_ref, spec_ref, skip_ref, fwd_ref, inv_ref, o_ref, *, n):
    shape = v_ref.shape
    row = lax.broadcasted_iota(jnp.int32, shape, 0)

    def conv3(x_ref, p):
        x = x_ref[...]
        prev = jnp.where(row == 0, 0.0, pltpu.roll(x, 1, 0))
        nxt = jnp.where(row == n - 1, 0.0, pltpu.roll(x, n - 1, 0))
        return prev * cw_ref[p, 0:1, :] + x * cw_ref[p, 1:2, :] + nxt * cw_ref[p, 2:3, :]

    z = conv3(v_ref, 2)
    fwd, inv = fwd_ref[...], inv_ref[...]
    for o, gate_ref in enumerate((x1_ref, x2_ref)):
        u = jnp.dot(fwd, _bf(z), preferred_element_type=F32)
        yr, yi = _spec_mul(u, spec_ref[o], n, row)
        y = jnp.dot(inv, _bf(jnp.concatenate([yr, yi], axis=0)), preferred_element_type=F32)
        z = conv3(gate_ref, o) * (y + z * skip_ref[o:o + 1, :])
    o_ref[...] = z.astype(o_ref.dtype)


def _hyena(proj_r, conv_w, spec, skip, fwd, inv, *, n_seq, seq_len, row0, tc=256):
    assert row0 % seq_len == 0
    rb = row0 // seq_len
    xb = lambda part: pl.BlockSpec((seq_len, tc), lambda b, j: (b + rb, (R_XC + part * C_CH) // tc + j))
    cw = conv_w.reshape(SHORT_CONV, 3, C_CH).transpose(1, 0, 2)
    return pl.pallas_call(
        functools.partial(_hyena_kernel, n=seq_len),
        out_shape=jax.ShapeDtypeStruct((n_seq * seq_len, C_CH), BF16),
        grid=(n_seq, C_CH // tc),
        in_specs=[xb(0), xb(1), xb(2),
                  pl.BlockSpec((3, SHORT_CONV, tc), lambda b, j: (0, 0, j)),
                  pl.BlockSpec((HYENA_ORDER, 2 * seq_len, tc), lambda b, j: (0, 0, j)),
                  pl.BlockSpec((HYENA_ORDER, tc), lambda b, j: (0, j)),
                  pl.BlockSpec((2 * seq_len, seq_len), lambda b, j: (0, 0)),
                  pl.BlockSpec((seq_len, 2 * seq_len), lambda b, j: (0, 0))],
        out_specs=pl.BlockSpec((seq_len, tc), lambda b, j: (b, j)),
        compiler_params=_cparams(),
        name=f"hyena_L{seq_len}",
    )(proj_r, proj_r, proj_r, cw, spec, skip, fwd, inv)


def _merge_kernel(oa_ref, ob_ref, oc_ref, wa_ref, wb_ref, wc_ref, ga_ref, gb_ref, gc_ref, o_ref,
                  wa_bf, wb_bf, wc_bf):
    @pl.when(pl.program_id(1) == 0)
    def _():
        wa_bf[...] = _bf(wa_ref[...])
        wb_bf[...] = _bf(wb_ref[...])
        wc_bf[...] = _bf(wc_ref[...])

    d = functools.partial(jnp.dot, preferred_element_type=F32)
    acc = _sigmoid(ga_ref[...]) * d(oa_ref[...], wa_bf[...])
    acc += _sigmoid(gb_ref[...]) * d(ob_ref[...], wb_bf[...])
    acc += _sigmoid(gc_ref[...]) * d(oc_ref[...], wc_bf[...])
    o_ref[...] = acc.astype(o_ref.dtype)


def _merge(o_a, o_b, o_c, w_a, w_b, w_c, proj_r, layer, tm=512, tn=512):
    t = o_a.shape[0]
    d = D_MODEL
    gate = lambda part: pl.BlockSpec((tm, tn), lambda j, i: (i, (R_GATES + part * d) // tn + j))
    act = lambda width: pl.BlockSpec((tm, width), lambda j, i: (i, 0))
    wgt = lambda width: _layer_spec((width, tn), lambda j, i: (0, j), layer)
    return pl.pallas_call(
        _merge_kernel,
        out_shape=jax.ShapeDtypeStruct((t, d), BF16),
        grid=(d // tn, t // tm),
        in_specs=[act(W_A), act(W_B), act(C_CH), wgt(W_A), wgt(W_B), wgt(C_CH),
                  gate(0), gate(1), gate(2)],
        out_specs=pl.BlockSpec((tm, tn), lambda j, i: (i, j)),
        scratch_shapes=[pltpu.VMEM((W_A, tn), BF16), pltpu.VMEM((W_B, tn), BF16), pltpu.VMEM((C_CH, tn), BF16)],
        compiler_params=_cparams(),
        name="merge",
    )(o_a, o_b, o_c, w_a, w_b, w_c, proj_r, proj_r, proj_r)


def _mm_resid_kernel(y_ref, w_ref, x_ref, gate_ref, o_ref, w_bf):
    @pl.when(pl.program_id(1) == 0)
    def _():
        w_bf[...] = _bf(w_ref[...])

    o_ref[...] = x_ref[...] + gate_ref[...] * jnp.dot(y_ref[...], w_bf[...], preferred_element_type=F32)


def _matmul_residual(y, w, x, mod4, gate_idx, layer, *, n_ctx_rows, rows_per_latent, tm=512, tn=512):
    t, kdim = y.shape
    n = w.shape[-1]
    assert y.dtype == BF16
    row = lambda i: _mod_row(i * tm, n_ctx_rows, rows_per_latent)
    return pl.pallas_call(
        _mm_resid_kernel,
        out_shape=jax.ShapeDtypeStruct((t, n), F32),
        grid=(n // tn, t // tm),
        in_specs=[pl.BlockSpec((tm, kdim), lambda j, i: (i, 0)),
                  _layer_spec((kdim, tn), lambda j, i: (0, j), layer),
                  pl.BlockSpec((tm, tn), lambda j, i: (i, j)),
                  pl.BlockSpec((None, None, 1, tn), lambda j, i: (row(i), gate_idx, 0, j))],
        out_specs=pl.BlockSpec((tm, tn), lambda j, i: (i, j)),
        scratch_shapes=[pltpu.VMEM((kdim, tn), BF16)],
        compiler_params=_cparams(),
        name="out_proj_residual",
    )(y, w, x, mod4)


def _router_kernel(x_ref, g_ref, scale_ref, shift_ref, wr_ref, br_ref, h_ref, ti_ref, tw_ref):
    x = x_ref[...]
    y = x * lax.rsqrt(jnp.mean(x * x, axis=-1, keepdims=True) + EPS)
    h = y * g_ref[...] * (1.0 + scale_ref[...]) + shift_ref[...]
    half = h.shape[1] // 2
    h_ref[...] = _pack_bf16_pair(h[:, :half], h[:, half:])
    logits = _dot_hl(h, wr_ref[...]) + br_ref[...]
    lane_i = lax.broadcasted_iota(jnp.int32, logits.shape, 1)
    lane = lane_i.astype(F32)
    neg = jnp.float32(-jnp.inf)
    cur = jnp.where(lane_i < N_EXPERTS, logits, neg)
    vals = []
    ti = jnp.zeros(logits.shape, F32)
    for kk in range(TOP_K):
        m = jnp.max(cur, axis=-1, keepdims=True)
        idx = jnp.min(jnp.where(cur == m, lane, float(LANES)), axis=-1, keepdims=True)
        ti = jnp.where(lane_i == kk, idx, ti)
        vals.append(m)
        cur = jnp.where(lane == idx, neg, cur)
    es = [jnp.exp(vk - vals[0]) for vk in vals]
    den = es[0] + es[1] + es[2] + es[3]
    tw = jnp.zeros(logits.shape, F32)
    for kk in range(TOP_K):
        tw = jnp.where(lane_i == kk, es[kk] / den, tw)
    ti_ref[...] = ti.astype(jnp.int32)
    tw_ref[...] = tw


def _router(x, g, mod4, shift_idx, scale_idx, w_router, b_router, *, n_ctx_rows, rows_per_latent, tm=256):
    t, d = x.shape
    row = lambda i: _mod_row(i * tm, n_ctx_rows, rows_per_latent)
    wr = jnp.pad(w_router, ((0, 0), (0, LANES - N_EXPERTS)))
    br = jnp.pad(b_router.reshape(1, -1), ((0, 0), (0, LANES - N_EXPERTS)))
    return pl.pallas_call(
        _router_kernel,
        out_shape=(jax.ShapeDtypeStruct((t, d // 2), jnp.uint32),
                   jax.ShapeDtypeStruct((t, LANES), jnp.int32),
                   jax.ShapeDtypeStruct((t, LANES), F32)),
        grid=(t // tm,),
        in_specs=[pl.BlockSpec((tm, d), lambda i: (i, 0)),
                  pl.BlockSpec((1, d), lambda i: (0, 0)),
                  pl.BlockSpec((None, None, 1, d), lambda i: (row(i), scale_idx, 0, 0)),
                  pl.BlockSpec((None, None, 1, d), lambda i: (row(i), shift_idx, 0, 0)),
                  pl.BlockSpec((d, LANES), lambda i: (0, 0)),
                  pl.BlockSpec((1, LANES), lambda i: (0, 0))],
        out_specs=(pl.BlockSpec((tm, d // 2), lambda i: (i, 0)),
                   pl.BlockSpec((tm, LANES), lambda i: (i, 0)),
                   pl.BlockSpec((tm, LANES), lambda i: (i, 0))),
        compiler_params=_cparams(),
        name="router",
    )(x, g.reshape(1, d), mod4, mod4, wr, br)


def _row_copy(src_hbm, dst_vmem, sem, src_row, dst_row):
    return pltpu.make_async_copy(src_hbm.at[pl.ds(src_row, 1), :], dst_vmem.at[pl.ds(dst_row, 1), :], sem)


def _gather_kernel(ta_ref, src_ref, nxt_ref, h_hbm, o_ref, buf, sem, *, tm):
    i = pl.program_id(0)
    last = pl.num_programs(0) - 1
    slot = i % 2

    unroll = 8
    stride = tm // unroll

    def fetch(idx_ref, s):
        def start(a, _):
            for b in range(unroll):
                r = b * stride + a
                _row_copy(h_hbm, buf.at[s], sem.at[s], idx_ref[0, r], r).start()
            return 0

        lax.fori_loop(0, stride, start, 0)

    def drain(s):
        def wait(a, _):
            for b in range(unroll):
                r = b * stride + a
                _row_copy(h_hbm, buf.at[s], sem.at[s], src_ref[0, r], r).wait()
            return 0

        lax.fori_loop(0, stride, wait, 0)
        lo, hi = _unpack_bf16_pair(buf[s])
        half = lo.shape[1]
        o_ref[:, :half] = lo.astype(o_ref.dtype)
        o_ref[:, half:] = hi.astype(o_ref.dtype)

    @pl.when(jnp.logical_and(i == 0, ta_ref[0] == 1))
    def _():
        fetch(src_ref, 0)

    next_active = jnp.logical_and(i < last, ta_ref[jnp.minimum(i + 1, last)] == 1)
    for s in (0, 1):
        @pl.when(jnp.logical_and(next_active, slot == 1 - s))
        def _():
            fetch(nxt_ref, s)

    for s in (0, 1):
        @pl.when(jnp.logical_and(ta_ref[i] == 1, slot == s))
        def _():
            drain(s)

    @pl.when(ta_ref[i] == 0)
    def _():
        o_ref[...] = jnp.zeros_like(o_ref)


def _moe_gather(h_packed, src_token, tile_active, n_rows):
    h = h_packed
    d = 2 * h.shape[1]
    tm = MOE_TM
    n_tiles = n_rows // tm
    src3 = src_token.reshape(n_tiles, 1, tm)
    return pl.pallas_call(
        functools.partial(_gather_kernel, tm=tm),
        out_shape=jax.ShapeDtypeStruct((n_rows, d), BF16),
        grid_spec=pltpu.PrefetchScalarGridSpec(
            num_scalar_prefetch=1,
            grid=(n_tiles,),
            in_specs=[pl.BlockSpec((None, 1, tm), lambda i, ta: (i, 0, 0), memory_space=pltpu.SMEM),
                      pl.BlockSpec((None, 1, tm), lambda i, ta: (jnp.minimum(i + 1, n_tiles - 1), 0, 0),
                                   memory_space=pltpu.SMEM),
                      pl.BlockSpec(memory_space=pl.ANY)],
            out_specs=pl.BlockSpec((tm, d), lambda i, ta: (i, 0)),
            scratch_shapes=[pltpu.VMEM((2, tm, d // 2), jnp.uint32), pltpu.SemaphoreType.DMA((2,))]),
        compiler_params=_cparams(dimension_semantics=("arbitrary",)),
        name="moe_gather",
    )(tile_active, src3, src3, h)


def _moe_up_kernel(te_ref, ta_ref, ts_ref, x_ref, wg_ref, wu_ref, bg_ref, bu_ref, rw_ref, o_ref):
    i = pl.program_id(1)

    @pl.when(ta_ref[i] == 1)
    def _():
        x = x_ref[...]
        gate = _dot(x, wg_ref[...]) + bg_ref[...]
        up = _dot(x, wu_ref[...]) + bu_ref[...]
        gate = jnp.minimum(gate, SWIGLU_LIMIT)
        up = jnp.clip(up, -SWIGLU_LIMIT, SWIGLU_LIMIT)
        act = (up + 1.0) * gate * _sigmoid(SWIGLU_ALPHA * gate)
        o_ref[...] = (act * rw_ref[...]).astype(o_ref.dtype)

    @pl.when(ta_ref[i] == 0)
    def _():
        o_ref[...] = jnp.zeros_like(o_ref)


def _moe_up(x_sorted, w_gu, b_gu, row_w, tile_expert, tile_active, tile_src, layer):
    p, d = x_sorted.shape
    tm, tf = MOE_TM, MOE_TF
    nf = D_FF // tf
    b3 = b_gu.reshape(N_EXPERTS, 1, 2 * D_FF)
    return pl.pallas_call(
        _moe_up_kernel,
        out_shape=jax.ShapeDtypeStruct((p, D_FF), BF16),
        grid_spec=pltpu.PrefetchScalarGridSpec(
            num_scalar_prefetch=3,
            grid=(nf, p // tm),
            in_specs=[pl.BlockSpec((tm, d), lambda j, i, te, ta, ts: (ts[i], 0)),
                      pl.BlockSpec((None, None, d, tf), lambda j, i, te, ta, ts: (layer, te[i], 0, j)),
                      pl.BlockSpec((None, None, d, tf), lambda j, i, te, ta, ts: (layer, te[i], 0, nf + j)),
                      pl.BlockSpec((None, 1, tf), lambda j, i, te, ta, ts: (te[i], 0, j)),
                      pl.BlockSpec((None, 1, tf), lambda j, i, te, ta, ts: (te[i], 0, nf + j)),
                      pl.BlockSpec((tm, 1), lambda j, i, te, ta, ts: (ts[i], 0))],
            out_specs=pl.BlockSpec((tm, tf), lambda j, i, te, ta, ts: (i, j)),
            scratch_shapes=[]),
        compiler_params=_cparams(),
        name="moe_up",
    )(tile_expert, tile_active, tile_src, x_sorted, w_gu, w_gu, b3, b3, row_w)


def _pack_bf16_pair(lo, hi):
    lo_bits = lax.bitcast_convert_type(_bf(lo).astype(F32), jnp.uint32)
    hi_bits = lax.bitcast_convert_type(_bf(hi).astype(F32), jnp.uint32)
    return hi_bits | (lo_bits >> 16)


def _unpack_bf16_pair(u):
    lo = lax.bitcast_convert_type(u << 16, F32)
    hi = lax.bitcast_convert_type(u & jnp.uint32(0xFFFF0000), F32)
    return lo, hi


def _moe_down_kernel(te_ref, ta_ref, ts_ref, a_ref, wlo_ref, whi_ref, blo_ref, bhi_ref, rw_ref, o_ref,
                     wlo_s, whi_s):
    i = pl.program_id(1)
    changed = jnp.logical_or(i == 0, te_ref[i] != te_ref[jnp.maximum(i - 1, 0)])

    @pl.when(changed)
    def _():
        wlo_s[...] = _bf(wlo_ref[...])
        whi_s[...] = _bf(whi_ref[...])

    @pl.when(ta_ref[i] == 1)
    def _():
        a = a_ref[...]
        rw = rw_ref[...]
        y_lo = jnp.dot(a, wlo_s[...], preferred_element_type=F32) + rw * blo_ref[...]
        y_hi = jnp.dot(a, whi_s[...], preferred_element_type=F32) + rw * bhi_ref[...]
        o_ref[...] = _pack_bf16_pair(y_lo, y_hi)

    @pl.when(ta_ref[i] == 0)
    def _():
        o_ref[...] = jnp.zeros_like(o_ref)


def _moe_down(act, w_down, b_down, row_w, tile_expert, tile_active, tile_src, layer, tn=1024):
    p, f = act.shape
    d = w_down.shape[-1]
    half = d // 2
    nh = half // tn
    tm = MOE_TM
    b3 = b_down.reshape(N_EXPERTS, 1, d)
    wspec = lambda off: pl.BlockSpec((None, None, f, tn), lambda j, i, te, ta, ts: (layer, te[i], 0, off + j))
    bspec = lambda off: pl.BlockSpec((None, 1, tn), lambda j, i, te, ta, ts: (te[i], 0, off + j))
    return pl.pallas_call(
        _moe_down_kernel,
        out_shape=jax.ShapeDtypeStruct((p, half), jnp.uint32),
        grid_spec=pltpu.PrefetchScalarGridSpec(
            num_scalar_prefetch=3,
            grid=(nh, p // tm),
            in_specs=[pl.BlockSpec((tm, f), lambda j, i, te, ta, ts: (ts[i], 0)),
                      wspec(0), wspec(nh), bspec(0), bspec(nh),
                      pl.BlockSpec((tm, 1), lambda j, i, te, ta, ts: (ts[i], 0))],
            out_specs=pl.BlockSpec((tm, tn), lambda j, i, te, ta, ts: (i, j)),
            scratch_shapes=[pltpu.VMEM((f, tn), BF16), pltpu.VMEM((f, tn), BF16)]),
        compiler_params=_cparams(),
        name="moe_down",
    )(tile_expert, tile_active, tile_src, act, w_down, w_down, b3, b3, row_w)


def _combine_kernel(dest_ref, nxt_ref, y_hbm, x_ref, gate_ref, fg_ref, o_ref, buf, sem, *, tm, final_norm):
    i = pl.program_id(0)
    last = pl.num_programs(0) - 1
    slot = i % 2

    def fetch(idx_ref, s):
        def start(r, _):
            for kk in range(TOP_K):
                _row_copy(y_hbm, buf.at[s, kk], sem.at[s], idx_ref[0, r * TOP_K + kk], r).start()
            return 0

        lax.fori_loop(0, tm, start, 0)

    def finish(s):
        def wait(r, _):
            for kk in range(TOP_K):
                _row_copy(y_hbm, buf.at[s, kk], sem.at[s], dest_ref[0, r * TOP_K + kk], r).wait()
            return 0

        lax.fori_loop(0, tm, wait, 0)
        parts = [_unpack_bf16_pair(buf[s, kk]) for kk in range(TOP_K)]
        half = buf.shape[-1]
        xs = []
        for side, cols in enumerate((slice(0, half), slice(half, 2 * half))):
            y = (parts[0][side] + parts[1][side]) + (parts[2][side] + parts[3][side])
            xs.append(x_ref[:, cols] + gate_ref[:, cols] * y)
        if final_norm:
            ms = (jnp.sum(xs[0] * xs[0], axis=-1, keepdims=True)
                  + jnp.sum(xs[1] * xs[1], axis=-1, keepdims=True)) / (2 * half)
            inv = lax.rsqrt(ms + EPS)
            xs = [xs[0] * inv * fg_ref[:, :half], xs[1] * inv * fg_ref[:, half:]]
        o_ref[:, :half] = xs[0]
        o_ref[:, half:] = xs[1]

    @pl.when(i == 0)
    def _():
        fetch(dest_ref, 0)

    for s in (0, 1):
        @pl.when(jnp.logical_and(i < last, slot == 1 - s))
        def _():
            fetch(nxt_ref, s)

    for s in (0, 1):
        @pl.when(slot == s)
        def _():
            finish(s)


def _moe_combine(y_sorted, dest, x, mod4, gate_idx, final_g, *, final_norm, n_ctx_rows, rows_per_latent):
    t, d = x.shape
    tm = COMBINE_TM
    row = lambda i: _mod_row(i * tm, n_ctx_rows, rows_per_latent)
    n_tiles = t // tm
    dest3 = dest.reshape(n_tiles, 1, tm * TOP_K)
    return pl.pallas_call(
        functools.partial(_combine_kernel, tm=tm, final_norm=final_norm),
        out_shape=jax.ShapeDtypeStruct((t, d), F32),
        grid=(n_tiles,),
        in_specs=[pl.BlockSpec((None, 1, tm * TOP_K), lambda i: (i, 0, 0), memory_space=pltpu.SMEM),
                  pl.BlockSpec((None, 1, tm * TOP_K), lambda i: (jnp.minimum(i + 1, n_tiles - 1), 0, 0),
                               memory_space=pltpu.SMEM),
                  pl.BlockSpec(memory_space=pl.ANY),
                  pl.BlockSpec((tm, d), lambda i: (i, 0)),
                  pl.BlockSpec((None, None, 1, d), lambda i: (row(i), gate_idx, 0, 0)),
                  pl.BlockSpec((1, d), lambda i: (0, 0))],
        out_specs=pl.BlockSpec((tm, d), lambda i: (i, 0)),
        scratch_shapes=[pltpu.VMEM((2, TOP_K, tm, d // 2), jnp.uint32), pltpu.SemaphoreType.DMA((2,))],
        compiler_params=_cparams(dimension_semantics=("arbitrary",)),
        name="moe_combine",
    )(dest3, dest3, y_sorted, x, mod4, final_g.reshape(1, d))


def _moe_plan(top_i, n_tiles):
    t = top_i.shape[0]
    tm = MOE_TM
    e_flat = top_i.reshape(-1)
    onehot = (e_flat[:, None] == jnp.arange(N_EXPERTS, dtype=jnp.int32)[None, :]).astype(jnp.int32)
    blk = 128
    within = jnp.cumsum(onehot.reshape(-1, blk, N_EXPERTS), axis=1)
    totals = within[:, -1, :]
    csum = (within + (jnp.cumsum(totals, axis=0) - totals)[:, None, :]).reshape(-1, N_EXPERTS)
    counts = csum[-1]
    rank = jnp.sum(onehot * csum, axis=1) - 1
    tiles_per = (counts + tm - 1) // tm
    tile_end = jnp.cumsum(tiles_per)
    group_row0 = (tile_end - tiles_per) * tm
    dest = group_row0[e_flat] + rank
    n_used = tile_end[-1]
    tile_ids = jnp.arange(n_tiles, dtype=jnp.int32)
    tile_expert = jnp.searchsorted(tile_end, tile_ids, side="right").astype(jnp.int32)
    tile_active = (tile_ids < n_used).astype(jnp.int32)
    last_expert = jnp.searchsorted(tile_end, n_used - 1, side="right").astype(jnp.int32)
    tile_expert = jnp.where(tile_active == 1, tile_expert, last_expert)
    tile_src = jnp.minimum(tile_ids, n_used - 1)
    return dest.astype(jnp.int32), tile_expert, tile_active, tile_src


def _moe(x, norm_g, mod4, p, l, final_g, *, final_norm, n_ctx_rows, rows_per_latent):
    t = x.shape[0]
    blk = dict(n_ctx_rows=n_ctx_rows, rows_per_latent=rows_per_latent)
    h2, top_i, top_w = _router(x, norm_g, mod4, 3, 4, p["w_router"][l], p["b_router"][l], **blk)
    top_i, top_w = top_i[:, :TOP_K], top_w[:, :TOP_K]
    n_rows = t * TOP_K + N_EXPERTS * MOE_TM
    dest, tile_expert, tile_active, tile_src = _moe_plan(top_i, n_rows // MOE_TM)
    slot_of_row = jnp.full((n_rows,), -1, jnp.int32).at[dest].set(jnp.arange(t * TOP_K, dtype=jnp.int32))
    real = slot_of_row >= 0
    src_token = jnp.where(real, slot_of_row // TOP_K, jnp.arange(n_rows, dtype=jnp.int32) % t)
    row_w = jnp.where(real, top_w.reshape(-1)[jnp.maximum(slot_of_row, 0)], 0.0).reshape(n_rows, 1)
    x_sorted = _moe_gather(h2, src_token, tile_active, n_rows)
    act = _moe_up(x_sorted, p["w_gu"], p["b_gu"][l], row_w, tile_expert, tile_active, tile_src, l)
    y_sorted = _moe_down(act, p["w_down"], p["b_down"][l], row_w, tile_expert, tile_active, tile_src, l)
    return _moe_combine(y_sorted, dest, x, mod4, 5, final_g, final_norm=final_norm, **blk)


def _trunk(x, cvec, p, final_g, caches, *, n_ctx, ctx_len, n_lat, lat_len):
    n_ctx_rows = n_ctx * ctx_len
    blk = dict(n_ctx_rows=n_ctx_rows, rows_per_latent=lat_len)
    rope = _rope_tables(lat_len)
    dft = {}
    for n in (ctx_len, lat_len):
        fwd, inv = _dft_tables(n)
        fwd_hi = _bf(fwd)
        dft[n] = (fwd_hi, _bf(fwd - fwd_hi.astype(F32)), _bf(inv))
    silu_c = jax.nn.silu(cvec)
    w_in_t = jnp.swapaxes(p["w_in"], 1, 2)
    outs = []
    for l in range(DEPTH):
        mod = _matmul(silu_c, p["w_mod"], n=N_MOD * D_MODEL, bias=p["b_mod"][l],
                      tm=16, tn=2048, tk=1024, name="modulation", layer=l)
        mod4 = mod.reshape(16, N_MOD, 1, D_MODEL)
        h = _adaln(x, p["norm1_g"][l], mod4, 0, 1, **blk)
        in_proj = functools.partial(_matmul_nt, h, w_in_t, layer=l)
        proj_a = in_proj(n=OFF_AB, row0=0, tm=512, tn=768, name="in_proj_a")
        ab = in_proj(n=LANES, row0=OFF_AB, tm=1024, tn=LANES, name="in_proj_ab")
        proj_r = in_proj(n=N_REST, row0=OFF_REST, tm=512, tn=768, name="in_proj_rest")

        ck, cv, s0f, s0b = caches[l]
        gates = _gdn_gates(ab, p["a_log"][l], p["dt_bias"][l])
        zeros_state = jnp.zeros((n_ctx, H_A, DK_A, DK_A), F32)
        gdn = functools.partial(_gdn, proj_a, gates, p["conv_a"][l], p["onorm_a"][l])
        oa_c, sf_c, sb_c = gdn(zeros_state, zeros_state, n_seq=n_ctx, seq_len=ctx_len, row0=0)
        oa_l, _, _ = gdn(s0f, s0b, n_seq=n_lat, seq_len=lat_len, row0=n_ctx_rows)

        lam_init = 0.8 - 0.6 * math.exp(-0.3 * l)
        attn = functools.partial(_attention, proj_r, p["lam"][l], p["subln_b"][l], lam_init)
        ob_c = attn(n_seq=n_ctx, seq_len=ctx_len, row0=0)
        ob_l = attn(n_seq=n_lat, seq_len=lat_len, row0=n_ctx_rows, rope=rope, cache=(ck, cv))

        oc = []
        for n_seq, n, row0 in ((n_ctx, ctx_len, 0), (n_lat, lat_len, n_ctx_rows)):
            fwd_hi, fwd_lo, inv = dft[n]
            spec = _hyena_filters(n, p["filt_w1"][l], p["filt_b1"][l], p["filt_freq"][l],
                                  p["filt_w2"][l], p["filt_b2"][l], p["filt_w3"][l], fwd_hi, fwd_lo)
            oc.append(_hyena(proj_r, p["conv_c"][l], spec, p["filt_skip"][l], fwd_hi, inv,
                             n_seq=n_seq, seq_len=n, row0=row0))

        o_a = jnp.concatenate([oa_c, oa_l], axis=0)
        o_b = jnp.concatenate([ob_c, ob_l], axis=0)
        o_c = jnp.concatenate(oc, axis=0)
        merged = _merge(o_a, o_b, o_c, p["w_br_a"], p["w_br_b"], p["w_br_c"], proj_r, l)
        x = _matmul_residual(merged, p["w_out"], x, mod4, 2, l, **blk)
        x = _moe(x, p["norm2_g"][l], mod4, p, l, final_g, final_norm=(l == DEPTH - 1), **blk)

        kv = proj_r[:n_ctx_rows, R_KB:R_XC].reshape(n_ctx, ctx_len, 2, H_B, DV_B)
        outs.append((kv[:, :, 0], kv[:, :, 1], sf_c, sb_c))
    return x, outs


def kernel(x_prompt, x_sample, cache_k, cache_v, state_fwd, state_bwd, c, c_ctx, norm1_g, norm2_g, final_g, w_mod, b_mod, w_in, conv_a, a_log, dt_bias, onorm_a, lam, subln_b, conv_c, filt_w1, filt_b1, filt_freq, filt_w2, filt_b2, filt_w3, filt_skip, w_br_a, w_br_b, w_br_c, w_out, w_router, b_router, w_gu, b_gu, w_down, b_down):
    p = dict(norm1_g=norm1_g, norm2_g=norm2_g, w_mod=w_mod, b_mod=b_mod, w_in=w_in, conv_a=conv_a,
             a_log=a_log, dt_bias=dt_bias, onorm_a=onorm_a, lam=lam, subln_b=subln_b, conv_c=conv_c,
             filt_w1=filt_w1, filt_b1=filt_b1, filt_freq=filt_freq, filt_w2=filt_w2,
             filt_b2=filt_b2, filt_w3=filt_w3, filt_skip=filt_skip, w_br_a=w_br_a,
             w_br_b=w_br_b, w_br_c=w_br_c, w_out=w_out, w_router=w_router, b_router=b_router,
             w_gu=w_gu, b_gu=b_gu, w_down=w_down, b_down=b_down)
    n_ctx, ctx_len, d = x_prompt.shape
    n_lat, lat_len, _ = x_sample.shape
    past = cache_k.shape[2]
    x = jnp.concatenate([x_prompt.reshape(n_ctx * ctx_len, d), x_sample.reshape(n_lat * lat_len, d)], axis=0)
    cvec = jnp.concatenate([c_ctx[None, :], c, jnp.zeros((16 - 1 - n_lat, d), F32)], axis=0)
    caches = [(cache_k[:, l].reshape(n_lat, past, W_B), cache_v[:, l].reshape(n_lat, past, W_B),
               state_fwd[:, l], state_bwd[:, l]) for l in range(DEPTH)]
    y, outs = _trunk(x, cvec, p, final_g, caches, n_ctx=n_ctx, ctx_len=ctx_len, n_lat=n_lat, lat_len=lat_len)
    y_prompt = y[:n_ctx * ctx_len].reshape(n_ctx, ctx_len, d)
    y_sample = y[n_ctx * ctx_len:].reshape(n_lat, lat_len, d)
    stack = lambda idx: jnp.stack([o[idx] for o in outs], axis=1)
    return (y_prompt, y_sample, stack(0), stack(1), stack(2), stack(3))
```
